```python
import jax, jax.numpy as jnp
from jax import lax
import numpy as np


D_MODEL = 2048
BATCH = 1
SEQ = 16384
DEPTH = 1
DEC_BATCH = 1
DEC_SEQ = 8192
PAST_LEN = 128

HEAD_DIM = 64
D_MIX = D_MODEL
N_HEADS_A = D_MIX // (2 * HEAD_DIM)
N_KV_A = N_HEADS_A // 4
N_HEADS_B = D_MIX // (2 * HEAD_DIM)
WIN = 128
ATT_BLOCK = 128
GRID_W = 64
NA_ROWS_MAX = 8
NA_COLS = 16
N_GROUPS = 4
EXPERTS_PER_GROUP = 8
N_EXPERTS = N_GROUPS * EXPERTS_PER_GROUP
TOP_K = 2
D_EXPERT = D_MODEL // 2
MOE_BLOCK = 128
EPS = 1e-6

QA_W = N_HEADS_A * HEAD_DIM
KVA_W = N_KV_A * HEAD_DIM
B_W = N_HEADS_B * HEAD_DIM
SPLIT_POINTS = (QA_W, QA_W + KVA_W, QA_W + 2 * KVA_W, QA_W + 2 * KVA_W + B_W, QA_W + 2 * KVA_W + 2 * B_W)
D_IN = QA_W + 2 * KVA_W + 3 * B_W

kernel_name = 'hymba_swa_natten_hmoe_encoder'


def _normalize(x):
    xf = x.astype(jnp.float32)
    return xf * lax.rsqrt(jnp.mean(xf * xf, axis=-1, keepdims=True) + EPS)


def rms_norm(x, g):
    return (_normalize(x) * g.astype(jnp.float32)).astype(x.dtype)


def window_attention(q, k, v, sink):
    b, t, h, dh = q.shape
    g = k.shape[2]
    rep = h // g
    L = ATT_BLOCK
    nb = t // L
    pad = ((0, 0), (L, L), (0, 0), (0, 0))
    kp = jnp.pad(k, pad).reshape(b, nb + 2, L, g, dh)
    vp = jnp.pad(v, pad).reshape(b, nb + 2, L, g, dh)
    kw = jnp.concatenate([kp[:, :-2], kp[:, 1:-1], kp[:, 2:]], axis=2)
    vw = jnp.concatenate([vp[:, :-2], vp[:, 1:-1], vp[:, 2:]], axis=2)
    qb = q.reshape(b, nb, L, g, rep, dh)
    s = jnp.einsum('bnqgrd,bnkgd->bngrqk', qb, kw, preferred_element_type=jnp.float32) * (dh ** -0.5)
    absdist = jnp.abs(jnp.arange(L)[:, None] - jnp.arange(3 * L)[None, :] + L)
    kpos = jnp.arange(nb)[:, None] * L - L + jnp.arange(3 * L)[None, :]
    valid = (absdist <= WIN)[None] & ((kpos >= 0) & (kpos < t))[:, None, :]
    slopes = jnp.exp2(-8.0 * (jnp.arange(h, dtype=jnp.float32) + 1.0) / h).reshape(g, rep)
    alibi = -slopes[:, :, None, None] * absdist.astype(jnp.float32)
    s = jnp.where(valid[None, :, None, None], s + alibi[None, None], -jnp.inf)
    sink_l = sink.astype(jnp.float32).reshape(1, 1, g, rep, 1, 1)
    m = jnp.maximum(jnp.max(s, axis=-1, keepdims=True), sink_l)
    p = jnp.exp(s - m)
    p = p / (jnp.sum(p, axis=-1, keepdims=True) + jnp.exp(sink_l - m))
    o = jnp.einsum('bngrqk,bnkgd->bnqgrd', p.astype(v.dtype), vw)
    return o.reshape(b, t, h * dh)


def neighborhood_attention(q, k, v, rpb):
    b, t, h, dh = q.shape
    rows = t // GRID_W
    kr = min(NA_ROWS_MAX, rows)
    qg = q.reshape(b, rows, GRID_W, h, dh)
    kg = k.reshape(b, rows, GRID_W, h, dh)
    vg = v.reshape(b, rows, GRID_W, h, dh)
    row_start = jnp.clip(jnp.arange(rows) - kr // 2, 0, rows - kr)
    col = jnp.arange(GRID_W)
    col_start = jnp.clip(col - NA_COLS // 2, 0, GRID_W - NA_COLS)
    col_valid = (col[None, :] >= col_start[:, None]) & (col[None, :] < col_start[:, None] + NA_COLS)
    dc_idx = jnp.clip(col[None, :] - col[:, None] + NA_COLS - 1, 0, 2 * NA_COLS - 2)
    scale = dh ** -0.5

    def one_row(r):
        rs = row_start[r]
        q_r = lax.dynamic_index_in_dim(qg, r, axis=1, keepdims=False)
        k_r = lax.dynamic_slice_in_dim(kg, rs, kr, axis=1)
        v_r = lax.dynamic_slice_in_dim(vg, rs, kr, axis=1)
        bias_rows = lax.dynamic_slice_in_dim(rpb, rs - r + NA_ROWS_MAX - 1, kr, axis=1)
        bias = bias_rows[:, :, dc_idx].transpose(0, 2, 1, 3).astype(jnp.float32)
        s = jnp.einsum('bqhd,bkchd->bhqkc', q_r, k_r, preferred_element_type=jnp.float32) * scale + bias[None]
        s = jnp.where(col_valid[:, None, :], s, -jnp.inf)
        p = jax.nn.softmax(s, axis=(-2, -1))
        return jnp.einsum('bhqkc,bkchd->bqhd', p.astype(v.dtype), v_r)

    out = lax.map(one_row, jnp.arange(rows))
    return out.transpose(1, 0, 2, 3, 4).reshape(b, t, h * dh)


def hierarchical_moe(x, w_router_group, w_router_expert, w_gate, w_up, w_down):
    b, t, d = x.shape
    n = b * t
    xf = x.reshape(n, d)
    g_logits = jnp.matmul(xf, w_router_group).astype(jnp.float32)
    p_grp = jax.nn.softmax(g_logits, axis=-1)
    g_idx = jnp.argmax(g_logits, axis=-1)
    p_g = jnp.take_along_axis(p_grp, g_idx[:, None], axis=1)[:, 0]
    e_logits = jnp.matmul(xf, w_router_expert).astype(jnp.float32).reshape(n, N_GROUPS, EXPERTS_PER_GROUP)
    e_sel = jnp.take_along_axis(e_logits, g_idx[:, None, None], axis=1)[:, 0]
    top_v, top_i = lax.top_k(jax.nn.softmax(e_sel, axis=-1), TOP_K)
    gate = p_g[:, None] * top_v / jnp.sum(top_v, axis=-1, keepdims=True)
    eid = g_idx[:, None] * EXPERTS_PER_GROUP + top_i

    nk = n * TOP_K
    flat_e = eid.reshape(-1)
    flat_tok = jnp.repeat(jnp.arange(n, dtype=jnp.int32), TOP_K)
    flat_w = gate.reshape(-1)
    order = jnp.argsort(flat_e)
    se = flat_e[order]
    counts = jnp.bincount(flat_e, length=N_EXPERTS)
    starts = jnp.cumsum(counts) - counts
    padded = ((counts + MOE_BLOCK - 1) // MOE_BLOCK) * MOE_BLOCK
    pend = jnp.cumsum(padded)
    pstart = pend - padded
    dest = pstart[se] + jnp.arange(nk) - starts[se]
    p_slots = ((nk + MOE_BLOCK - 1) // MOE_BLOCK) * MOE_BLOCK + N_EXPERTS * MOE_BLOCK
    n_blk = p_slots // MOE_BLOCK
    slot_tok = jnp.full((p_slots,), n, dtype=jnp.int32).at[dest].set(flat_tok[order])
    slot_w = jnp.zeros((p_slots,), jnp.float32).at[dest].set(flat_w[order])
    blk_start = jnp.arange(n_blk) * MOE_BLOCK
    blk_e = jnp.minimum(jnp.sum(pend[None, :] <= blk_start[:, None], axis=1), N_EXPERTS - 1)
    x_ext = jnp.concatenate([xf, jnp.zeros((1, d), xf.dtype)], axis=0)
    xs = x_ext[slot_tok].reshape(n_blk, MOE_BLOCK, d)

    def run_block(args):
        xb, e = args
        hid = jax.nn.silu(jnp.matmul(xb, w_gate[e])) * jnp.matmul(xb, w_up[e])
        return jnp.matmul(hid, w_down[e])

    ys = lax.map(run_block, (xs, blk_e)).reshape(p_slots, d)
    y = jax.ops.segment_sum(ys * slot_w[:, None].astype(ys.dtype), slot_tok, num_segments=n + 1)[:n]
    return y.reshape(b, t, d)


def encoder_layer(x, g_attn, w_in, g_mix, sink_a, rpb_b, w_out, g_ffn,
                  w_router_group, w_router_expert, w_gate, w_up, w_down):
    b, t, _ = x.shape
    h = rms_norm(x, g_attn)
    proj = jnp.matmul(h, w_in)
    qa, ka, va, qb, kb, vb = jnp.split(proj, SPLIT_POINTS, axis=-1)
    o_a = window_attention(qa.reshape(b, t, N_HEADS_A, HEAD_DIM),
                           ka.reshape(b, t, N_KV_A, HEAD_DIM),
                           va.reshape(b, t, N_KV_A, HEAD_DIM), sink_a)
    o_b = neighborhood_attention(qb.reshape(b, t, N_HEADS_B, HEAD_DIM),
                                 kb.reshape(b, t, N_HEADS_B, HEAD_DIM),
                                 vb.reshape(b, t, N_HEADS_B, HEAD_DIM), rpb_b)
    mixed = (jnp.concatenate([_normalize(o_a), _normalize(o_b)], axis=-1) * g_mix.astype(jnp.float32)).astype(x.dtype)
    x = x + jnp.matmul(mixed, w_out)
    x = x + hierarchical_moe(rms_norm(x, g_ffn), w_router_group, w_router_expert, w_gate, w_up, w_down)
    return x


def setup_inputs(seed: int = 0) -> dict:
    key = jax.random.key(seed)
    ks = jax.random.split(key, 16)
    f32 = jnp.float32
    nrm = jax.random.normal
    return {
        'x_prompt': nrm(ks[0], (BATCH, SEQ, D_MODEL), f32),
        'x_sample': nrm(ks[1], (DEC_BATCH, DEC_SEQ, D_MODEL), f32),
        'g_attn': 1.0 + 0.01 * nrm(ks[2], (DEPTH, D_MODEL), f32),
        'w_in': nrm(ks[3], (DEPTH, D_MODEL, D_IN), f32) * D_MODEL ** -0.5,
        'g_mix': 1.0 + 0.01 * nrm(ks[4], (DEPTH, D_MIX), f32),
        'sink_a': 0.5 * nrm(ks[5], (DEPTH, N_HEADS_A), f32),
        'rpb_b': 0.1 * nrm(ks[6], (DEPTH, N_HEADS_B, 2 * NA_ROWS_MAX - 1, 2 * NA_COLS - 1), f32),
        'w_out': nrm(ks[7], (DEPTH, D_MIX, D_MODEL), f32) * D_MIX ** -0.5,
        'g_ffn': 1.0 + 0.01 * nrm(ks[8], (DEPTH, D_MODEL), f32),
        'w_router_group': nrm(ks[9], (DEPTH, D_MODEL, N_GROUPS), f32) * D_MODEL ** -0.5,
        'w_router_expert': nrm(ks[10], (DEPTH, D_MODEL, N_EXPERTS), f32) * D_MODEL ** -0.5,
        'w_gate': nrm(ks[11], (DEPTH, N_EXPERTS, D_MODEL, D_EXPERT), f32) * D_MODEL ** -0.5,
        'w_up': nrm(ks[12], (DEPTH, N_EXPERTS, D_MODEL, D_EXPERT), f32) * D_MODEL ** -0.5,
        'w_down': nrm(ks[13], (DEPTH, N_EXPERTS, D_EXPERT, D_MODEL), f32) * D_EXPERT ** -0.5,
        'g_final': 1.0 + 0.01 * nrm(ks[14], (D_MODEL,), f32),
    }


def reference(x_prompt, x_sample, g_attn, w_in, g_mix, sink_a, rpb_b, w_out, g_ffn,
              w_router_group, w_router_expert, w_gate, w_up, w_down, g_final):
    y_prompt = x_prompt
    y_sample = x_sample
    for l in range(DEPTH):
        layer_params = (g_attn[l], w_in[l], g_mix[l], sink_a[l], rpb_b[l], w_out[l], g_ffn[l],
                        w_router_group[l], w_router_expert[l], w_gate[l], w_up[l], w_down[l])
        y_prompt = encoder_layer(y_prompt, *layer_params)
        y_sample = encoder_layer(y_sample, *layer_params)
    y_prompt = rms_norm(y_prompt, g_final)
    y_sample = rms_norm(y_sample, g_final)
    return (y_prompt, y_sample)
```

```python
import functools

import jax
import jax.numpy as jnp
from jax import lax
from jax.experimental import pallas as pl
from jax.experimental.pallas import tpu as pltpu

F32 = jnp.float32
BF16 = jnp.bfloat16
I32 = jnp.int32

D_MODEL = 2048
HEAD_DIM = 64
N_HEADS_A = 16
N_KV_A = 4
N_HEADS_B = 16
WIN = 128
GRID_W = 64
NA_ROWS = 8
NA_COLS = 16
N_GROUPS = 4
EPG = 8
N_EXPERTS = 32
D_EXPERT = 1024
EPS = 1e-6

QA_W = N_HEADS_A * HEAD_DIM
KVA_W = N_KV_A * HEAD_DIM
B_W = N_HEADS_B * HEAD_DIM
D_IN = QA_W + 2 * KVA_W + 3 * B_W
HEAD_GROUP_W = 256
Q_SCALE = HEAD_DIM ** -0.5

ROW_TILE = 512
NA_ROWS_PER_STEP = 8
MOE_BLK = 256
DISPATCH_TILE = 512
ROUTER_ROWS = 40
VMEM_LIMIT = 56 * 1024 * 1024


def _cparams(sem, vmem=None):
    return pltpu.CompilerParams(dimension_semantics=sem, vmem_limit_bytes=vmem)


def _rms_scale(x):
    return x * lax.rsqrt(jnp.mean(x * x, axis=-1, keepdims=True) + EPS)


def _in_proj_body(x_ref, g_ref, w_ref, o_ref):
    h = (_rms_scale(x_ref[...]) * g_ref[...]).astype(BF16)
    nch = 512
    for j in range(D_IN // nch):
        sl = slice(j * nch, (j + 1) * nch)
        o_ref[:, sl] = jnp.dot(h, w_ref[:, sl], preferred_element_type=F32).astype(BF16)


def _in_proj_kernel(xp_ref, xs_ref, g_ref, w_ref, o_ref, *, n_prompt_blocks):
    i = pl.program_id(0)

    @pl.when(i < n_prompt_blocks)
    def _():
        _in_proj_body(xp_ref, g_ref, w_ref, o_ref)

    @pl.when(i >= n_prompt_blocks)
    def _():
        _in_proj_body(xs_ref, g_ref, w_ref, o_ref)


def _in_proj(xp, xs, g_attn, w_in_bf16):
    tp, ts = xp.shape[0], xs.shape[0]
    npb, nsb = tp // ROW_TILE, ts // ROW_TILE
    return pl.pallas_call(
        functools.partial(_in_proj_kernel, n_prompt_blocks=npb),
        grid=(npb + nsb,),
        in_specs=[
            pl.BlockSpec((ROW_TILE, D_MODEL), lambda i: (jnp.minimum(i, npb - 1), 0)),
            pl.BlockSpec((ROW_TILE, D_MODEL), lambda i: (jnp.maximum(i - npb, 0), 0)),
            pl.BlockSpec((1, D_MODEL), lambda i: (0, 0)),
            pl.BlockSpec((D_MODEL, D_IN), lambda i: (0, 0), pipeline_mode=pl.Buffered(1)),
        ],
        out_specs=pl.BlockSpec((ROW_TILE, D_IN), lambda i: (i, 0)),
        out_shape=jax.ShapeDtypeStruct((tp + ts, D_IN), BF16),
        compiler_params=_cparams(("arbitrary",), VMEM_LIMIT),
        name="in_proj",
    )(xp, xs, g_attn.reshape(1, D_MODEL), w_in_bf16)


def _seq_bounds(pos, bounds):
    lo = jnp.int32(bounds[0])
    hi = jnp.int32(bounds[1])
    for k in range(1, len(bounds) - 1):
        inside = pos >= bounds[k]
        lo = jnp.where(inside, bounds[k], lo)
        hi = jnp.where(inside, bounds[k + 1], hi)
    return lo, hi


def _lane_head(width):
    return lax.broadcasted_iota(I32, (1, width), 1) // HEAD_DIM


def _stack_heads(x, lane_head):
    zero = jnp.zeros_like(x)
    return jnp.concatenate([jnp.where(lane_head == r, x, zero) for r in range(4)], axis=0)


def _diag_heads(o_all, m, lane_head):
    out = jnp.where(lane_head == 0, o_all[0:m], 0.0)
    for r in range(1, 4):
        out = out + jnp.where(lane_head == r, o_all[r * m:(r + 1) * m], 0.0)
    return out


_NT = (((1,), (1,)), ((), ()))


def _win_kernel(sink_ref, q_ref, kp_ref, km_ref, kn_ref, vp_ref, vm_ref, vn_ref, o_ref,
                kbuf, vbuf, *, bounds):
    i = pl.program_id(0)
    tq = q_ref.shape[0]
    tok0 = i * tq
    lo, hi = _seq_bounds(tok0, bounds)
    lane_head = _lane_head(HEAD_GROUP_W)
    n_heads = N_HEADS_A

    qi = lax.broadcasted_iota(I32, (WIN, 3 * WIN), 0)
    ki = lax.broadcasted_iota(I32, (WIN, 3 * WIN), 1)
    dist_i = jnp.abs(qi - ki + WIN)
    dist = dist_i.astype(F32)
    band = dist_i <= WIN
    krow = lax.broadcasted_iota(I32, (1, 3 * WIN), 1)
    er = lax.broadcasted_iota(I32, (HEAD_GROUP_W, HEAD_GROUP_W), 0)
    ec = lax.broadcasted_iota(I32, (HEAD_GROUP_W, HEAD_GROUP_W), 1)

    for g in range(N_KV_A):
        spread = jnp.where((er // HEAD_DIM == g) & (er % HEAD_DIM == ec % HEAD_DIM), 1.0, 0.0).astype(BF16)
        for buf, (p_ref, m_ref, n_ref) in ((kbuf, (kp_ref, km_ref, kn_ref)), (vbuf, (vp_ref, vm_ref, vn_ref))):
            buf[0:WIN, :] = jnp.dot(p_ref[...], spread, preferred_element_type=F32).astype(BF16)
            buf[WIN:WIN + tq, :] = jnp.dot(m_ref[...], spread, preferred_element_type=F32).astype(BF16)
            buf[WIN + tq:, :] = jnp.dot(n_ref[...], spread, preferred_element_type=F32).astype(BF16)

        def sub_block(c, carry, g=g):
            r0 = pl.multiple_of(c * WIN, WIN)
            qg = q_ref[pl.ds(r0, WIN), g * HEAD_GROUP_W:(g + 1) * HEAD_GROUP_W] * Q_SCALE
            lhs = _stack_heads(qg, lane_head)
            kwin = kbuf[pl.ds(r0, 3 * WIN), :]
            vwin = vbuf[pl.ds(r0, 3 * WIN), :]
            s = lax.dot_general(lhs, kwin, _NT, preferred_element_type=F32)
            kpos = tok0 + r0 - WIN + krow
            valid = band & (kpos >= lo) & (kpos < hi)
            probs = []
            for r in range(4):
                h = g * 4 + r
                slope = 2.0 ** (-8.0 * (h + 1.0) / n_heads)
                sink = sink_ref[h]
                t = s[r * WIN:(r + 1) * WIN] - slope * dist
                t = jnp.where(valid, t, -jnp.inf)
                m = jnp.maximum(jnp.max(t, axis=-1, keepdims=True), sink)
                p = jnp.exp(t - m)
                den = jnp.sum(p, axis=-1, keepdims=True) + jnp.exp(sink - m)
                probs.append((p * (1.0 / den)).astype(BF16))
            p_all = jnp.concatenate(probs, axis=0)
            o_all = jnp.dot(p_all, vwin, preferred_element_type=F32)
            o_ref[pl.ds(r0, WIN), g * HEAD_GROUP_W:(g + 1) * HEAD_GROUP_W] = _diag_heads(o_all, WIN, lane_head)
            return carry

        lax.fori_loop(0, tq // WIN, sub_block, 0)


def _window_attention(proj, sink, bounds):
    t_all = proj.shape[0]
    tq = ROW_TILE
    nblk = t_all // tq
    halo_per_tile = tq // WIN
    n_halo = t_all // WIN
    k_col = (QA_W + 3 * B_W) // KVA_W
    v_col = k_col + 1

    def main_spec(col):
        return pl.BlockSpec((tq, KVA_W), lambda i: (i, col))

    def prev_spec(col):
        return pl.BlockSpec((WIN, KVA_W), lambda i: (jnp.maximum(i * halo_per_tile - 1, 0), col))

    def next_spec(col):
        return pl.BlockSpec((WIN, KVA_W), lambda i: (jnp.minimum((i + 1) * halo_per_tile, n_halo - 1), col))

    return pl.pallas_call(
        functools.partial(_win_kernel, bounds=bounds),
        grid=(nblk,),
        in_specs=[
            pl.BlockSpec(memory_space=pltpu.SMEM),
            pl.BlockSpec((tq, QA_W), lambda i: (i, 0)),
            prev_spec(k_col), main_spec(k_col), next_spec(k_col),
            prev_spec(v_col), main_spec(v_col), next_spec(v_col),
        ],
        out_specs=pl.BlockSpec((tq, QA_W), lambda i: (i, 0)),
        out_shape=jax.ShapeDtypeStruct((t_all, QA_W), F32),
        scratch_shapes=[pltpu.VMEM((tq + 2 * WIN, KVA_W), BF16), pltpu.VMEM((tq + 2 * WIN, KVA_W), BF16)],
        compiler_params=_cparams(("arbitrary",), VMEM_LIMIT),
        name="window_attention",
    )(sink, proj, proj, proj, proj, proj, proj, proj)


def _bias_table_kernel(rpb_ref, o_ref):
    j0 = pl.program_id(0)
    g = pl.program_id(1)
    n_rel_rows = 2 * NA_ROWS - 1
    n_rel_cols = 2 * NA_COLS - 1
    qc = lax.broadcasted_iota(I32, (GRID_W, 2 * GRID_W), 0)
    lane = lax.broadcasted_iota(I32, (GRID_W, 2 * GRID_W), 1)
    kc = lane % GRID_W
    second = lane >= GRID_W
    rel = jnp.clip(kc - qc + NA_COLS - 1, 0, n_rel_cols - 1)
    col_start = jnp.clip(qc - NA_COLS // 2, 0, GRID_W - NA_COLS)
    col_valid = (kc >= col_start) & (kc < col_start + NA_COLS)
    for r in range(4):
        h = g * 4 + r
        for kp in range(NA_ROWS // 2):
            base_a = (h * n_rel_rows + j0 + 2 * kp) * n_rel_cols
            base_b = base_a + n_rel_cols

            def pick(d, acc, base_a=base_a, base_b=base_b):
                val = jnp.where(second, rpb_ref[base_b + d], rpb_ref[base_a + d])
                return jnp.where(rel == d, val, acc)

            tile = lax.fori_loop(0, n_rel_cols, pick, jnp.zeros((GRID_W, 2 * GRID_W), F32))
            o_ref[r * GRID_W:(r + 1) * GRID_W, kp * 2 * GRID_W:(kp + 1) * 2 * GRID_W] = jnp.where(col_valid, tile, -jnp.inf)


def _bias_table(rpb):
    n_j0 = NA_ROWS
    return pl.pallas_call(
        _bias_table_kernel,
        grid=(n_j0, 4),
        in_specs=[pl.BlockSpec(memory_space=pltpu.SMEM)],
        out_specs=pl.BlockSpec((None, None, 4 * GRID_W, NA_ROWS * GRID_W), lambda j, g: (j, g, 0, 0)),
        out_shape=jax.ShapeDtypeStruct((n_j0, 4, 4 * GRID_W, NA_ROWS * GRID_W), F32),
        compiler_params=_cparams(("arbitrary", "arbitrary")),
        name="na_bias_table",
    )(rpb.reshape(-1))


def _na_kernel(q_ref, kp_ref, km_ref, kn_ref, vp_ref, vm_ref, vn_ref, bt_ref, o_ref, kbuf, vbuf, *, row_bounds):
    i = pl.program_id(1)
    rows = NA_ROWS_PER_STEP
    tile = rows * GRID_W
    lane_head = _lane_head(HEAD_GROUP_W)
    for buf, (p_ref, m_ref, n_ref) in ((kbuf, (kp_ref, km_ref, kn_ref)), (vbuf, (vp_ref, vm_ref, vn_ref))):
        buf[0:tile, :] = p_ref[...]
        buf[tile:2 * tile, :] = m_ref[...]
        buf[2 * tile:3 * tile, :] = n_ref[...]
    row0 = i * rows
    lo, hi = _seq_bounds(row0, row_bounds)

    def one_row(jr, carry):
        r = row0 + jr
        rs = jnp.clip(r - NA_ROWS // 2, lo, hi - NA_ROWS)
        j0 = rs - r + NA_ROWS - 1
        off = pl.multiple_of((rs - row0 + rows) * GRID_W, GRID_W)
        q0 = pl.multiple_of(jr * GRID_W, GRID_W)
        q = q_ref[pl.ds(q0, GRID_W), :] * Q_SCALE
        lhs = _stack_heads(q, lane_head)
        kwin = kbuf[pl.ds(off, NA_ROWS * GRID_W), :]
        vwin = vbuf[pl.ds(off, NA_ROWS * GRID_W), :]
        s = lax.dot_general(lhs, kwin, _NT, preferred_element_type=F32) + bt_ref[j0]
        m = jnp.max(s, axis=-1, keepdims=True)
        p = jnp.exp(s - m)
        den = jnp.sum(p, axis=-1, keepdims=True)
        pb = (p * (1.0 / den)).astype(BF16)
        o_all = jnp.dot(pb, vwin, preferred_element_type=F32)
        o_ref[pl.ds(q0, GRID_W), :] = _diag_heads(o_all, GRID_W, lane_head)
        return carry

    lax.fori_loop(0, rows, one_row, 0)


def _neighborhood_attention(proj, bias_table, bounds):
    t_all = proj.shape[0]
    tile = NA_ROWS_PER_STEP * GRID_W
    nblk = t_all // tile
    row_bounds = tuple(b // GRID_W for b in bounds)
    q_col = QA_W // HEAD_GROUP_W
    k_col = (QA_W + B_W) // HEAD_GROUP_W
    v_col = (QA_W + 2 * B_W) // HEAD_GROUP_W

    def spec(col, shift):
        return pl.BlockSpec((tile, HEAD_GROUP_W), lambda g, i: (jnp.clip(i + shift, 0, nblk - 1), col + g))

    return pl.pallas_call(
        functools.partial(_na_kernel, row_bounds=row_bounds),
        grid=(4, nblk),
        in_specs=[
            spec(q_col, 0),
            spec(k_col, -1), spec(k_col, 0), spec(k_col, 1),
            spec(v_col, -1), spec(v_col, 0), spec(v_col, 1),
            pl.BlockSpec((NA_ROWS, None, 4 * GRID_W, NA_ROWS * GRID_W), lambda g, i: (0, g, 0, 0)),
        ],
        out_specs=pl.BlockSpec((tile, HEAD_GROUP_W), lambda g, i: (i, g)),
        out_shape=jax.ShapeDtypeStruct((t_all, B_W), F32),
        scratch_shapes=[pltpu.VMEM((3 * tile, HEAD_GROUP_W), BF16), pltpu.VMEM((3 * tile, HEAD_GROUP_W), BF16)],
        compiler_params=_cparams(("arbitrary", "arbitrary"), VMEM_LIMIT),
        name="neighborhood_attention",
    )(proj, proj, proj, proj, proj, proj, proj, bias_table)


def _split_bf16(x):
    hi = x.astype(BF16)
    lo = (x - hi.astype(F32)).astype(BF16)
    return hi, lo


def _out_proj_body(x_ref, oa_ref, ob_ref, gmix_ref, wout_ref, gffn_ref, wr_ref, x1_ref, h2_ref, lg_ref):
    ma = (_rms_scale(oa_ref[...]) * gmix_ref[:, :QA_W]).astype(BF16)
    mb = (_rms_scale(ob_ref[...]) * gmix_ref[:, QA_W:]).astype(BF16)
    y = jnp.dot(ma, wout_ref[:QA_W, :], preferred_element_type=F32)
    y = y + jnp.dot(mb, wout_ref[QA_W:, :], preferred_element_type=F32)
    x1 = x_ref[...] + y
    x1_ref[...] = x1
    h2 = _rms_scale(x1) * gffn_ref[...]
    h2_ref[...] = h2
    h_hi, h_lo = _split_bf16(h2)
    w_hi, w_lo = _split_bf16(wr_ref[...])
    lg = lax.dot_general(w_hi, h_hi, _NT, preferred_element_type=F32)
    lg = lg + lax.dot_general(w_lo, h_hi, _NT, preferred_element_type=F32)
    lg = lg + lax.dot_general(w_hi, h_lo, _NT, preferred_element_type=F32)
    lg_ref[...] = lg


def _out_proj_kernel(xp_ref, xs_ref, *rest, n_prompt_blocks):
    i = pl.program_id(0)

    @pl.when(i < n_prompt_blocks)
    def _():
        _out_proj_body(xp_ref, *rest)

    @pl.when(i >= n_prompt_blocks)
    def _():
        _out_proj_body(xs_ref, *rest)


def _out_proj(xp, xs, o_a, o_b, g_mix, w_out_bf16, g_ffn, w_router_t):
    tp, ts = xp.shape[0], xs.shape[0]
    t_all = tp + ts
    npb, nsb = tp // ROW_TILE, ts // ROW_TILE
    row = lambda i: (i, 0)
    const = lambda i: (0, 0)
    return pl.pallas_call(
        functools.partial(_out_proj_kernel, n_prompt_blocks=npb),
        grid=(npb + nsb,),
        in_specs=[
            pl.BlockSpec((ROW_TILE, D_MODEL), lambda i: (jnp.minimum(i, npb - 1), 0)),
            pl.BlockSpec((ROW_TILE, D_MODEL), lambda i: (jnp.maximum(i - npb, 0), 0)),
            pl.BlockSpec((ROW_TILE, QA_W), row),
            pl.BlockSpec((ROW_TILE, B_W), row),
            pl.BlockSpec((1, D_MODEL), const),
            pl.BlockSpec((D_MODEL, D_MODEL), const, pipeline_mode=pl.Buffered(1)),
            pl.BlockSpec((1, D_MODEL), const),
            pl.BlockSpec((ROUTER_ROWS, D_MODEL), const),
        ],
        out_specs=[
            pl.BlockSpec((ROW_TILE, D_MODEL), row),
            pl.BlockSpec((ROW_TILE, D_MODEL), row),
            pl.BlockSpec((ROUTER_ROWS, ROW_TILE), lambda i: (0, i)),
        ],
        out_shape=[
            jax.ShapeDtypeStruct((t_all, D_MODEL), F32),
            jax.ShapeDtypeStruct((t_all, D_MODEL), F32),
            jax.ShapeDtypeStruct((ROUTER_ROWS, t_all), F32),
        ],
        compiler_params=_cparams(("arbitrary",), VMEM_LIMIT),
        name="out_proj_router",
    )(xp, xs, o_a, o_b, g_mix.reshape(1, D_MODEL), w_out_bf16, g_ffn.reshape(1, D_MODEL), w_router_t)


def _route_kernel(l_ref, dest_ref, gate_ref, blk_ref, pad_ref, pref_scr, *, t_all, n_blk, n_slots):
    chunk = 256
    lanes = 128
    sub8 = lax.broadcasted_iota(I32, (EPG, t_all), 0)
    gl = jnp.where(sub8 < N_GROUPS, l_ref[N_EXPERTS:N_EXPERTS + EPG, :], -jnp.inf)
    gmax = jnp.max(gl, axis=0, keepdims=True)
    gidx = jnp.min(jnp.where(gl == gmax, sub8, EPG), axis=0, keepdims=True)
    p_grp = 1.0 / jnp.sum(jnp.exp(gl - gmax), axis=0, keepdims=True)

    e_sel = l_ref[(N_GROUPS - 1) * EPG:N_GROUPS * EPG, :]
    for g in range(N_GROUPS - 2, -1, -1):
        e_sel = jnp.where(gidx == g, l_ref[g * EPG:(g + 1) * EPG, :], e_sel)
    ee = jnp.exp(e_sel - jnp.max(e_sel, axis=0, keepdims=True))
    pe = ee * (1.0 / jnp.sum(ee, axis=0, keepdims=True))
    v1 = jnp.max(pe, axis=0, keepdims=True)
    i1 = jnp.min(jnp.where(pe == v1, sub8, EPG), axis=0, keepdims=True)
    pe2 = jnp.where(sub8 == i1, -1.0, pe)
    v2 = jnp.max(pe2, axis=0, keepdims=True)
    i2 = jnp.min(jnp.where(pe2 == v2, sub8, EPG), axis=0, keepdims=True)
    norm = p_grp * (1.0 / (v1 + v2))
    gate_ref[...] = jnp.where(sub8 == 0, v1 * norm, jnp.where(sub8 == 1, v2 * norm, 0.0))
    e1 = gidx * EPG + i1
    e2 = gidx * EPG + i2

    tri = jnp.where(lax.broadcasted_iota(I32, (chunk, chunk), 0) < lax.broadcasted_iota(I32, (chunk, chunk), 1),
                    1.0, 0.0).astype(BF16)
    sub32 = lax.broadcasted_iota(I32, (N_EXPERTS, chunk), 0)
    count = jnp.zeros((N_EXPERTS, 1), F32)
    for c in range(t_all // chunk):
        sl = slice(c * chunk, (c + 1) * chunk)
        member = jnp.where((sub32 == e1[:, sl]) | (sub32 == e2[:, sl]), 1.0, 0.0)
        pref_scr[:, sl] = jnp.dot(member.astype(BF16), tri, preferred_element_type=F32) + count
        count = count + jnp.sum(member, axis=1, keepdims=True)

    nblk_e = jnp.floor((count + (MOE_BLK - 1)) * (1.0 / MOE_BLK))
    nb_l = jnp.broadcast_to(nblk_e, (N_EXPERTS, lanes))
    hi16 = jnp.floor(nb_l * (1.0 / 16.0))
    lo16 = nb_l - 16.0 * hi16
    ltri = jnp.where(lax.broadcasted_iota(I32, (N_EXPERTS, N_EXPERTS), 1) < lax.broadcasted_iota(I32, (N_EXPERTS, N_EXPERTS), 0),
                     1.0, 0.0).astype(BF16)
    first_blk = (16.0 * jnp.dot(ltri, hi16.astype(BF16), preferred_element_type=F32)
                 + jnp.dot(ltri, lo16.astype(BF16), preferred_element_type=F32))
    end_blk = first_blk + nb_l
    pstart = first_blk[:, 0:1] * float(MOE_BLK)

    sub32t = lax.broadcasted_iota(I32, (N_EXPERTS, t_all), 0)
    slot = pref_scr[...] + pstart
    d1 = jnp.sum(jnp.where(sub32t == e1, slot, 0.0), axis=0, keepdims=True)
    d2 = jnp.sum(jnp.where(sub32t == e2, slot, 0.0), axis=0, keepdims=True)
    dest_ref[...] = jnp.where(sub8 == 0, d1, jnp.where(sub8 == 1, d2, 0.0)).astype(I32)

    blk_i = lax.broadcasted_iota(I32, (N_EXPERTS, n_blk), 1).astype(F32)
    owner = jnp.sum(jnp.where(end_blk[:, 0:1] <= blk_i, 1.0, 0.0), axis=0, keepdims=True)
    owner = jnp.minimum(owner, float(N_EXPERTS - 1))
    n_used = jnp.sum(nblk_e, axis=0, keepdims=True)
    row8 = lax.broadcasted_iota(I32, (8, n_blk), 0)
    blk_ref[...] = jnp.where(row8 == 0, owner, jnp.where(row8 == 1, n_used, 0.0)).astype(I32)
    pad_ref[0:N_EXPERTS, :] = (first_blk * float(MOE_BLK) + jnp.broadcast_to(count, (N_EXPERTS, lanes))).astype(I32)
    last = lax.broadcasted_iota(I32, (N_EXPERTS, lanes), 0) == N_EXPERTS - 1
    pad_ref[N_EXPERTS:, :] = jnp.where(last, float(n_slots), end_blk * float(MOE_BLK)).astype(I32)


def _route(logits_t, n_slots):
    t_all = logits_t.shape[1]
    n_blk = pl.cdiv(n_slots // MOE_BLK, 128) * 128
    vmem = pl.BlockSpec(memory_space=pltpu.VMEM)
    return pl.pallas_call(
        functools.partial(_route_kernel, t_all=t_all, n_blk=n_blk, n_slots=n_slots),
        in_specs=[vmem],
        out_specs=[vmem, vmem, vmem, vmem],
        out_shape=[
            jax.ShapeDtypeStruct((8, t_all), I32),
            jax.ShapeDtypeStruct((8, t_all), F32),
            jax.ShapeDtypeStruct((8, n_blk), I32),
            jax.ShapeDtypeStruct((2 * N_EXPERTS, 128), I32),
        ],
        scratch_shapes=[pltpu.VMEM((N_EXPERTS, t_all), F32)],
        compiler_params=pltpu.CompilerParams(vmem_limit_bytes=VMEM_LIMIT),
        name="route",
    )(logits_t)


def _row_copy(src_ref, src_row, dst_ref, dst_row, sem):
    return pltpu.make_async_copy(src_ref.at[pl.ds(src_row, 1), :], dst_ref.at[pl.ds(dst_row, 1), :], sem)


def _dispatch_kernel(d1_ref, d2_ref, padlo_ref, padhi_ref, h_ref, xs_ref, zero_ref, sem, zsem):
    i = pl.program_id(0)
    tm = h_ref.shape[0]
    tok0 = i * tm

    def issue(r, c):
        _row_copy(h_ref, r, xs_ref, d1_ref[tok0 + r], sem).start()
        _row_copy(h_ref, r, xs_ref, d2_ref[tok0 + r], sem).start()
        return c

    lax.fori_loop(0, tm, issue, 0)

    @pl.when(i == 0)
    def _():
        zero_ref[...] = jnp.zeros_like(zero_ref)

        def per_expert(e, c):
            def fill(s, c2):
                _row_copy(zero_ref, 0, xs_ref, s, zsem).start()
                return c2

            def drain(s, c2):
                _row_copy(zero_ref, 0, xs_ref, s, zsem).wait()
                return c2

            lax.fori_loop(padlo_ref[e], padhi_ref[e], fill, 0)
            lax.fori_loop(padlo_ref[e], padhi_ref[e], drain, 0)
            return c

        lax.fori_loop(0, N_EXPERTS, per_expert, 0)

    def drain_rows(r, c):
        _row_copy(h_ref, r, xs_ref, d1_ref[tok0 + r], sem).wait()
        _row_copy(h_ref, r, xs_ref, d2_ref[tok0 + r], sem).wait()
        return c

    lax.fori_loop(0, tm, drain_rows, 0)


def _dispatch(h2, d1, d2, pad_lo, pad_hi, n_slots):
    t_all = h2.shape[0]
    tm = DISPATCH_TILE
    return pl.pallas_call(
        _dispatch_kernel,
        grid_spec=pltpu.PrefetchScalarGridSpec(
            num_scalar_prefetch=4,
            grid=(t_all // tm,),
            in_specs=[pl.BlockSpec((tm, D_MODEL), lambda i, *_: (i, 0))],
            out_specs=pl.BlockSpec(memory_space=pl.ANY),
            scratch_shapes=[pltpu.VMEM((8, D_MODEL), F32), pltpu.SemaphoreType.DMA(()), pltpu.SemaphoreType.DMA(())],
        ),
        out_shape=jax.ShapeDtypeStruct((n_slots, D_MODEL), F32),
        compiler_params=pltpu.CompilerParams(dimension_semantics=("arbitrary",), has_side_effects=True),
        name="dispatch",
    )(d1, d2, pad_lo, pad_hi, h2)


def _expert_changed(b, blk_e_ref):
    prev = blk_e_ref[jnp.maximum(b - 1, 0)]
    return (b == 0) | (blk_e_ref[b] != prev)


def _moe_up_kernel(blk_e_ref, nused_ref, x_ref, wg_ref, wu_ref, hid_ref, wg_bf, wu_bf):
    b = pl.program_id(0)

    @pl.when(b < nused_ref[0])
    def _():
        @pl.when(_expert_changed(b, blk_e_ref))
        def _():
            wg_bf[...] = wg_ref[...].astype(BF16)
            wu_bf[...] = wu_ref[...].astype(BF16)

        x = x_ref[...].astype(BF16)
        nch = 512
        for j in range(D_EXPERT // nch):
            sl = slice(j * nch, (j + 1) * nch)
            gt = jnp.dot(x, wg_bf[:, sl], preferred_element_type=F32)
            up = jnp.dot(x, wu_bf[:, sl], preferred_element_type=F32)
            hid_ref[:, sl] = (gt * (1.0 / (1.0 + jnp.exp(-gt))) * up).astype(BF16)

    @pl.when(b >= nused_ref[0])
    def _():
        hid_ref[...] = jnp.zeros_like(hid_ref)


def _moe_down_kernel(blk_e_ref, nused_ref, hid_ref, wd_ref, y_ref, wd_bf):
    b = pl.program_id(0)

    @pl.when(b < nused_ref[0])
    def _():
        @pl.when(_expert_changed(b, blk_e_ref))
        def _():
            wd_bf[...] = wd_ref[...].astype(BF16)

        nch = 512
        for j in range(D_MODEL // nch):
            sl = slice(j * nch, (j + 1) * nch)
            y_ref[:, sl] = jnp.dot(hid_ref[...], wd_bf[:, sl], preferred_element_type=F32)

    @pl.when(b >= nused_ref[0])
    def _():
        y_ref[...] = jnp.zeros_like(y_ref)


def _blk(b, nused_ref):
    return jnp.minimum(b, nused_ref[0] - 1)


def _moe(xs, blk_e, n_used, w_gate, w_up, w_down):
    n_slots = xs.shape[0]
    n_blk = n_slots // MOE_BLK
    hid = pl.pallas_call(
        _moe_up_kernel,
        grid_spec=pltpu.PrefetchScalarGridSpec(
            num_scalar_prefetch=2,
            grid=(n_blk,),
            in_specs=[
                pl.BlockSpec((MOE_BLK, D_MODEL), lambda b, be, nu: (_blk(b, nu), 0)),
                pl.BlockSpec((None, D_MODEL, D_EXPERT), lambda b, be, nu: (be[_blk(b, nu)], 0, 0)),
                pl.BlockSpec((None, D_MODEL, D_EXPERT), lambda b, be, nu: (be[_blk(b, nu)], 0, 0)),
            ],
            out_specs=pl.BlockSpec((MOE_BLK, D_EXPERT), lambda b, be, nu: (b, 0)),
            scratch_shapes=[pltpu.VMEM((D_MODEL, D_EXPERT), BF16), pltpu.VMEM((D_MODEL, D_EXPERT), BF16)],
        ),
        out_shape=jax.ShapeDtypeStruct((n_slots, D_EXPERT), BF16),
        compiler_params=_cparams(("arbitrary",), VMEM_LIMIT),
        name="moe_gate_up",
    )(blk_e, n_used, xs, w_gate, w_up)
    return pl.pallas_call(
        _moe_down_kernel,
        grid_spec=pltpu.PrefetchScalarGridSpec(
            num_scalar_prefetch=2,
            grid=(n_blk,),
            in_specs=[
                pl.BlockSpec((MOE_BLK, D_EXPERT), lambda b, be, nu: (_blk(b, nu), 0)),
                pl.BlockSpec((None, D_EXPERT, D_MODEL), lambda b, be, nu: (be[_blk(b, nu)], 0, 0)),
            ],
            out_specs=pl.BlockSpec((MOE_BLK, D_MODEL), lambda b, be, nu: (b, 0)),
            scratch_shapes=[pltpu.VMEM((D_EXPERT, D_MODEL), BF16)],
        ),
        out_shape=jax.ShapeDtypeStruct((n_slots, D_MODEL), F32),
        compiler_params=_cparams(("arbitrary",), VMEM_LIMIT),
        name="moe_down",
    )(blk_e, n_used, hid, w_down)


def _combine_body(i, d1_ref, d2_ref, x1_ref, gate_ref, g_ref, ys_ref, out_ref, r1, r2, sem):
    tm = x1_ref.shape[0]
    tok0 = i * tm

    def issue(r, c):
        _row_copy(ys_ref, d1_ref[tok0 + r], r1, r, sem).start()
        _row_copy(ys_ref, d2_ref[tok0 + r], r2, r, sem).start()
        return c

    def drain(r, c):
        _row_copy(ys_ref, d1_ref[tok0 + r], r1, r, sem).wait()
        _row_copy(ys_ref, d2_ref[tok0 + r], r2, r, sem).wait()
        return c

    lax.fori_loop(0, tm, issue, 0)
    lax.fori_loop(0, tm, drain, 0)
    gt = gate_ref[...]
    x2 = x1_ref[...] + (r1[...] * gt[:, 0:1] + r2[...] * gt[:, 1:2])
    out_ref[...] = _rms_scale(x2) * g_ref[...]


def _combine_kernel(d1_ref, d2_ref, x1_ref, gate_ref, g_ref, ys_ref, outp_ref, outs_ref, r1, r2, sem, *, n_prompt_blocks):
    i = pl.program_id(0)

    @pl.when(i < n_prompt_blocks)
    def _():
        _combine_body(i, d1_ref, d2_ref, x1_ref, gate_ref, g_ref, ys_ref, outp_ref, r1, r2, sem)

    @pl.when(i >= n_prompt_blocks)
    def _():
        _combine_body(i, d1_ref, d2_ref, x1_ref, gate_ref, g_ref, ys_ref, outs_ref, r1, r2, sem)


def _combine(x1, gates_t, g_final, ys, d1, d2, tp, ts):
    tm = DISPATCH_TILE
    npb, nsb = tp // tm, ts // tm
    return pl.pallas_call(
        functools.partial(_combine_kernel, n_prompt_blocks=npb),
        grid_spec=pltpu.PrefetchScalarGridSpec(
            num_scalar_prefetch=2,
            grid=(npb + nsb,),
            in_specs=[
                pl.BlockSpec((tm, D_MODEL), lambda i, *_: (i, 0)),
                pl.BlockSpec((tm, 2), lambda i, *_: (i, 0)),
                pl.BlockSpec((1, D_MODEL), lambda i, *_: (0, 0)),
                pl.BlockSpec(memory_space=pl.ANY),
            ],
            out_specs=[
                pl.BlockSpec((tm, D_MODEL), lambda i, *_: (jnp.minimum(i, npb - 1), 0)),
                pl.BlockSpec((tm, D_MODEL), lambda i, *_: (jnp.maximum(i - npb, 0), 0)),
            ],
            scratch_shapes=[pltpu.VMEM((tm, D_MODEL), F32), pltpu.VMEM((tm, D_MODEL), F32), pltpu.SemaphoreType.DMA(())],
        ),
        out_shape=[jax.ShapeDtypeStruct((tp, D_MODEL), F32), jax.ShapeDtypeStruct((ts, D_MODEL), F32)],
        compiler_params=_cparams(("arbitrary",), VMEM_LIMIT),
        name="combine_final_norm",
    )(d1, d2, x1, gates_t, g_final.reshape(1, D_MODEL), ys)


def _layer(xp, xs, g_attn, w_in, g_mix, sink_a, rpb_b, w_out, g_ffn, w_rg, w_re, w_gate, w_up, w_down, g_final):
    tp, ts = xp.shape[0], xs.shape[0]
    t_all = tp + ts
    bounds = (0, tp, t_all)
    s0, s1, s2, s3, s4 = QA_W, QA_W + KVA_W, QA_W + 2 * KVA_W, QA_W + 2 * KVA_W + B_W, QA_W + 2 * KVA_W + 2 * B_W
    w_in_p = jnp.concatenate([w_in[:, :s0], w_in[:, s2:s3], w_in[:, s3:s4], w_in[:, s4:], w_in[:, s0:s1], w_in[:, s1:s2]],
                             axis=1).astype(BF16)
    w_router_t = jnp.concatenate([w_re.T, w_rg.T, jnp.zeros((ROUTER_ROWS - N_EXPERTS - N_GROUPS, D_MODEL), F32)], axis=0)

    proj = _in_proj(xp, xs, g_attn, w_in_p)
    o_a = _window_attention(proj, sink_a, bounds)
    o_b = _neighborhood_attention(proj, _bias_table(rpb_b), bounds)
    x1, h2, logits_t = _out_proj(xp, xs, o_a, o_b, g_mix, w_out.astype(BF16), g_ffn, w_router_t)

    n_slots = 2 * t_all + N_EXPERTS * MOE_BLK
    n_blk = n_slots // MOE_BLK
    dest, gate, blk, pad = _route(logits_t, n_slots)
    d1, d2 = dest[0], dest[1]
    xs_sorted = _dispatch(h2, d1, d2, pad[:N_EXPERTS, 0], pad[N_EXPERTS:, 0], n_slots)
    ys = _moe(xs_sorted, blk[0, :n_blk], blk[1, :1], w_gate, w_up, w_down)
    return _combine(x1, gate[:2].T, g_final, ys, d1, d2, tp, ts)


def kernel(x_prompt, x_sample, g_attn, w_in, g_mix, sink_a, rpb_b, w_out, g_ffn, w_router_group, w_router_expert,
           w_gate, w_up, w_down, g_final):
    assert x_prompt.shape[0] == 1 and x_sample.shape[0] == 1 and g_attn.shape[0] == 1
    yp, ys = _layer(x_prompt[0], x_sample[0], g_attn[0], w_in[0], g_mix[0], sink_a[0], rpb_b[0], w_out[0], g_ffn[0],
                    w_router_group[0], w_router_expert[0], w_gate[0], w_up[0], w_down[0], g_final)
    return yp[None], ys[None]
```

```python
import functools

import jax
import jax.numpy as jnp
from jax import lax
from jax.experimental import pallas as pl
from jax.experimental.pallas import tpu as pltpu

F32 = jnp.float32
BF16 = jnp.bfloat16
I32 = jnp.int32

D_MODEL = 2048
HEAD_DIM = 64
N_HEADS_A = 16
N_KV_A = 4
N_HEADS_B = 16
WIN = 128
GRID_W = 64
NA_ROWS = 8
NA_COLS = 16
N_GROUPS = 4
EPG = 8
N_EXPERTS = 32
D_EXPERT = 1024
EPS = 1e-6

QA_W = N_HEADS_A * HEAD_DIM
KVA_W = N_KV_A * HEAD_DIM
B_W = N_HEADS_B * HEAD_DIM
D_IN = QA_W + 2 * KVA_W + 3 * B_W
HEAD_GROUP_W = 256
Q_SCALE = HEAD_DIM ** -0.5

ROW_TILE = 512
NA_ROWS_PER_STEP = 8
MOE_BLK = 256
DISPATCH_TILE = 512
ROUTER_ROWS = 40
VMEM_LIMIT = 56 * 1024 * 1024


def _cparams(sem, vmem=None):
    return pltpu.CompilerParams(dimension_semantics=sem, vmem_limit_bytes=vmem)


def _rms_scale(x):
    return x * lax.rsqrt(jnp.mean(x * x, axis=-1, keepdims=True) + EPS)


def _in_proj_body(x_ref, g_ref, w_ref, o_ref):
    h = (_rms_scale(x_ref[...]) * g_ref[...]).astype(BF16)
    nch = 512
    for j in range(D_IN // nch):
        sl = slice(j * nch, (j + 1) * nch)
        o_ref[:, sl] = jnp.dot(h, w_ref[:, sl], preferred_element_type=F32).astype(BF16)


def _in_proj_kernel(xp_ref, xs_ref, g_ref, w_ref, o_ref, *, n_prompt_blocks):
    i = pl.program_id(0)

    @pl.when(i < n_prompt_blocks)
    def _():
        _in_proj_body(xp_ref, g_ref, w_ref, o_ref)

    @pl.when(i >= n_prompt_blocks)
    def _():
        _in_proj_body(xs_ref, g_ref, w_ref, o_ref)


def _in_proj(xp, xs, g_attn, w_in_bf16):
    tp, ts = xp.shape[0], xs.shape[0]
    npb, nsb = tp // ROW_TILE, ts // ROW_TILE
    return pl.pallas_call(
        functools.partial(_in_proj_kernel, n_prompt_blocks=npb),
        grid=(npb + nsb,),
        in_specs=[
            pl.BlockSpec((ROW_TILE, D_MODEL), lambda i: (jnp.minimum(i, npb - 1), 0)),
            pl.BlockSpec((ROW_TILE, D_MODEL), lambda i: (jnp.maximum(i - npb, 0), 0)),
            pl.BlockSpec((1, D_MODEL), lambda i: (0, 0)),
            pl.BlockSpec((D_MODEL, D_IN), lambda i: (0, 0), pipeline_mode=pl.Buffered(1)),
        ],
        out_specs=pl.BlockSpec((ROW_TILE, D_IN), lambda i: (i, 0)),
        out_shape=jax.ShapeDtypeStruct((tp + ts, D_IN), BF16),
        compiler_params=_cparams(("arbitrary",), VMEM_LIMIT),
        name="in_proj",
    )(xp, xs, g_attn.reshape(1, D_MODEL), w_in_bf16)


def _seq_bounds(pos, bounds):
    lo = jnp.int32(bounds[0])
    hi = jnp.int32(bounds[1])
    for k in range(1, len(bounds) - 1):
        inside = pos >= bounds[k]
        lo = jnp.where(inside, bounds[k], lo)
        hi = jnp.where(inside, bounds[k + 1], hi)
    return lo, hi


def _lane_head(width):
    return lax.broadcasted_iota(I32, (1, width), 1) // HEAD_DIM


def _stack_heads(x, lane_head):
    zero = jnp.zeros_like(x)
    return jnp.concatenate([jnp.where(lane_head == r, x, zero) for r in range(4)], axis=0)


def _diag_heads(o_all, m, lane_head):
    out = jnp.where(lane_head == 0, o_all[0:m], 0.0)
    for r in range(1, 4):
        out = out + jnp.where(lane_head == r, o_all[r * m:(r + 1) * m], 0.0)
    return out


_NT = (((1,), (1,)), ((), ()))


def _win_kernel(sink_ref, q_ref, kp_ref, km_ref, kn_ref, vp_ref, vm_ref, vn_ref, o_ref,
                kbuf, vbuf, bias_scr, *, bounds):
    i = pl.program_id(0)
    tq = q_ref.shape[0]
    n_sub = tq // WIN
    tok0 = i * tq
    lo, hi = _seq_bounds(tok0, bounds)
    lane_head = _lane_head(HEAD_GROUP_W)

    @pl.when(i == 0)
    def _():
        qi = lax.broadcasted_iota(I32, (WIN, 3 * WIN), 0)
        ki = lax.broadcasted_iota(I32, (WIN, 3 * WIN), 1)
        dist_i = jnp.abs(qi - ki + WIN)
        dist = dist_i.astype(F32)
        for h in range(N_HEADS_A):
            slope = 2.0 ** (-8.0 * (h + 1.0) / N_HEADS_A)
            base = jnp.where(dist_i <= WIN, -(slope * dist), -jnp.inf)
            bias_scr[0, h] = base
            bias_scr[1, h] = jnp.where(ki >= WIN, base, -jnp.inf)
            bias_scr[2, h] = jnp.where(ki < 2 * WIN, base, -jnp.inf)

    er = lax.broadcasted_iota(I32, (HEAD_GROUP_W, HEAD_GROUP_W), 0)
    ec = lax.broadcasted_iota(I32, (HEAD_GROUP_W, HEAD_GROUP_W), 1)
    for g in range(N_KV_A):
        spread = jnp.where((er // HEAD_DIM == g) & (er % HEAD_DIM == ec % HEAD_DIM), 1.0, 0.0).astype(BF16)
        for buf, (p_ref, m_ref, n_ref) in ((kbuf, (kp_ref, km_ref, kn_ref)), (vbuf, (vp_ref, vm_ref, vn_ref))):
            buf[g, 0:WIN, :] = jnp.dot(p_ref[...], spread, preferred_element_type=F32).astype(BF16)
            buf[g, WIN:WIN + tq, :] = jnp.dot(m_ref[...], spread, preferred_element_type=F32).astype(BF16)
            buf[g, WIN + tq:, :] = jnp.dot(n_ref[...], spread, preferred_element_type=F32).astype(BF16)

    for g in range(N_KV_A):
        for c in range(n_sub):
            r0 = c * WIN
            qg = q_ref[r0:r0 + WIN, g * HEAD_GROUP_W:(g + 1) * HEAD_GROUP_W] * Q_SCALE
            lhs = _stack_heads(qg, lane_head)
            kwin = kbuf[g, r0:r0 + 3 * WIN, :]
            vwin = vbuf[g, r0:r0 + 3 * WIN, :]
            s = lax.dot_general(lhs, kwin, _NT, preferred_element_type=F32)
            variant = 0
            if c == 0:
                variant = jnp.where(tok0 == lo, 1, 0)
            if c == n_sub - 1:
                variant = jnp.where(tok0 + tq == hi, 2, 0)
            probs = []
            for r in range(4):
                h = g * 4 + r
                sink = sink_ref[h]
                t = s[r * WIN:(r + 1) * WIN] + bias_scr[variant, h]
                m = jnp.maximum(jnp.max(t, axis=-1, keepdims=True), sink)
                p = jnp.exp(t - m)
                den = jnp.sum(p, axis=-1, keepdims=True) + jnp.exp(sink - m)
                probs.append((p * (1.0 / den)).astype(BF16))
            p_all = jnp.concatenate(probs, axis=0)
            o_all = jnp.dot(p_all, vwin, preferred_element_type=F32)
            o_ref[r0:r0 + WIN, g * HEAD_GROUP_W:(g + 1) * HEAD_GROUP_W] = _diag_heads(o_all, WIN, lane_head)


def _window_attention(proj, sink, bounds):
    t_all = proj.shape[0]
    tq = ROW_TILE
    nblk = t_all // tq
    halo_per_tile = tq // WIN
    n_halo = t_all // WIN
    k_col = (QA_W + 3 * B_W) // KVA_W
    v_col = k_col + 1

    def main_spec(col):
        return pl.BlockSpec((tq, KVA_W), lambda i: (i, col))

    def prev_spec(col):
        return pl.BlockSpec((WIN, KVA_W), lambda i: (jnp.maximum(i * halo_per_tile - 1, 0), col))

    def next_spec(col):
        return pl.BlockSpec((WIN, KVA_W), lambda i: (jnp.minimum((i + 1) * halo_per_tile, n_halo - 1), col))

    return pl.pallas_call(
        functools.partial(_win_kernel, bounds=bounds),
        grid=(nblk,),
        in_specs=[
            pl.BlockSpec(memory_space=pltpu.SMEM),
            pl.BlockSpec((tq, QA_W), lambda i: (i, 0)),
            prev_spec(k_col), main_spec(k_col), next_spec(k_col),
            prev_spec(v_col), main_spec(v_col), next_spec(v_col),
        ],
        out_specs=pl.BlockSpec((tq, QA_W), lambda i: (i, 0)),
        out_shape=jax.ShapeDtypeStruct((t_all, QA_W), F32),
        scratch_shapes=[pltpu.VMEM((N_KV_A, tq + 2 * WIN, KVA_W), BF16), pltpu.VMEM((N_KV_A, tq + 2 * WIN, KVA_W), BF16),
                        pltpu.VMEM((3, N_HEADS_A, WIN, 3 * WIN), F32)],
        compiler_params=_cparams(("arbitrary",), VMEM_LIMIT),
        name="window_attention",
    )(sink, proj, proj, proj, proj, proj, proj, proj)


def _bias_table_kernel(rpb_ref, o_ref):
    j0 = pl.program_id(0)
    g = pl.program_id(1)
    n_rel_rows = 2 * NA_ROWS - 1
    n_rel_cols = 2 * NA_COLS - 1
    qc = lax.broadcasted_iota(I32, (GRID_W, 2 * GRID_W), 0)
    lane = lax.broadcasted_iota(I32, (GRID_W, 2 * GRID_W), 1)
    kc = lane % GRID_W
    second = lane >= GRID_W
    rel = jnp.clip(kc - qc + NA_COLS - 1, 0, n_rel_cols - 1)
    col_start = jnp.clip(qc - NA_COLS // 2, 0, GRID_W - NA_COLS)
    col_valid = (kc >= col_start) & (kc < col_start + NA_COLS)
    for r in range(4):
        h = g * 4 + r
        for kp in range(NA_ROWS // 2):
            base_a = (h * n_rel_rows + j0 + 2 * kp) * n_rel_cols
            base_b = base_a + n_rel_cols

            tile = jnp.zeros((GRID_W, 2 * GRID_W), F32)
            for d in range(n_rel_cols):
                val = jnp.where(second, rpb_ref[base_b + d], rpb_ref[base_a + d])
                tile = jnp.where(rel == d, val, tile)
            o_ref[r * GRID_W:(r + 1) * GRID_W, kp * 2 * GRID_W:(kp + 1) * 2 * GRID_W] = jnp.where(col_valid, tile, -jnp.inf)


def _bias_table(rpb):
    n_j0 = NA_ROWS
    return pl.pallas_call(
        _bias_table_kernel,
        grid=(n_j0, 4),
        in_specs=[pl.BlockSpec(memory_space=pltpu.SMEM)],
        out_specs=pl.BlockSpec((None, None, 4 * GRID_W, NA_ROWS * GRID_W), lambda j, g: (j, g, 0, 0)),
        out_shape=jax.ShapeDtypeStruct((n_j0, 4, 4 * GRID_W, NA_ROWS * GRID_W), F32),
        compiler_params=_cparams(("arbitrary", "arbitrary")),
        name="na_bias_table",
    )(rpb.reshape(-1))


def _na_kernel(q_ref, kp_ref, km_ref, kn_ref, vp_ref, vm_ref, vn_ref, bt_ref, o_ref, kbuf, vbuf, *, row_bounds):
    i = pl.program_id(1)
    rows = NA_ROWS_PER_STEP
    tile = rows * GRID_W
    lane_head = _lane_head(HEAD_GROUP_W)
    for buf, (p_ref, m_ref, n_ref) in ((kbuf, (kp_ref, km_ref, kn_ref)), (vbuf, (vp_ref, vm_ref, vn_ref))):
        buf[0:tile, :] = p_ref[...]
        buf[tile:2 * tile, :] = m_ref[...]
        buf[2 * tile:3 * tile, :] = n_ref[...]
    row0 = i * rows
    lo, hi = _seq_bounds(row0, row_bounds)

    for jr in range(rows):
        r = row0 + jr
        rs = jnp.clip(r - NA_ROWS // 2, lo, hi - NA_ROWS)
        j0 = rs - r + NA_ROWS - 1
        off = pl.multiple_of((rs - row0 + rows) * GRID_W, GRID_W)
        q0 = jr * GRID_W
        q = q_ref[q0:q0 + GRID_W, :] * Q_SCALE
        lhs = _stack_heads(q, lane_head)
        kwin = kbuf[pl.ds(off, NA_ROWS * GRID_W), :]
        vwin = vbuf[pl.ds(off, NA_ROWS * GRID_W), :]
        s = lax.dot_general(lhs, kwin, _NT, preferred_element_type=F32) + bt_ref[j0]
        m = jnp.max(s, axis=-1, keepdims=True)
        p = jnp.exp(s - m)
        den = jnp.sum(p, axis=-1, keepdims=True)
        pb = (p * (1.0 / den)).astype(BF16)
        o_all = jnp.dot(pb, vwin, preferred_element_type=F32)
        o_ref[q0:q0 + GRID_W, :] = _diag_heads(o_all, GRID_W, lane_head)


def _neighborhood_attention(proj, bias_table, bounds):
    t_all = proj.shape[0]
    tile = NA_ROWS_PER_STEP * GRID_W
    nblk = t_all // tile
    row_bounds = tuple(b // GRID_W for b in bounds)
    q_col = QA_W // HEAD_GROUP_W
    k_col = (QA_W + B_W) // HEAD_GROUP_W
    v_col = (QA_W + 2 * B_W) // HEAD_GROUP_W

    def spec(col, shift):
        return pl.BlockSpec((tile, HEAD_GROUP_W), lambda g, i: (jnp.clip(i + shift, 0, nblk - 1), col + g))

    return pl.pallas_call(
        functools.partial(_na_kernel, row_bounds=row_bounds),
        grid=(4, nblk),
        in_specs=[
            spec(q_col, 0),
            spec(k_col, -1), spec(k_col, 0), spec(k_col, 1),
            spec(v_col, -1), spec(v_col, 0), spec(v_col, 1),
            pl.BlockSpec((NA_ROWS, None, 4 * GRID_W, NA_ROWS * GRID_W), lambda g, i: (0, g, 0, 0)),
        ],
        out_specs=pl.BlockSpec((tile, HEAD_GROUP_W), lambda g, i: (i, g)),
        out_shape=jax.ShapeDtypeStruct((t_all, B_W), F32),
        scratch_shapes=[pltpu.VMEM((3 * tile, HEAD_GROUP_W), BF16), pltpu.VMEM((3 * tile, HEAD_GROUP_W), BF16)],
        compiler_params=_cparams(("arbitrary", "arbitrary"), VMEM_LIMIT),
        name="neighborhood_attention",
    )(proj, proj, proj, proj, proj, proj, proj, bias_table)


def _split_bf16(x):
    hi = x.astype(BF16)
    lo = (x - hi.astype(F32)).astype(BF16)
    return hi, lo


def _out_proj_body(x_ref, oa_ref, ob_ref, gmix_ref, wout_ref, gffn_ref, wr_ref, x1_ref, h2_ref, lg_ref):
    ma = (_rms_scale(oa_ref[...]) * gmix_ref[:, :QA_W]).astype(BF16)
    mb = (_rms_scale(ob_ref[...]) * gmix_ref[:, QA_W:]).astype(BF16)
    y = jnp.dot(ma, wout_ref[:QA_W, :], preferred_element_type=F32)
    y = y + jnp.dot(mb, wout_ref[QA_W:, :], preferred_element_type=F32)
    x1 = x_ref[...] + y
    x1_ref[...] = x1
    h2 = _rms_scale(x1) * gffn_ref[...]
    h2_ref[...] = h2
    h_hi, h_lo = _split_bf16(h2)
    w_hi, w_lo = _split_bf16(wr_ref[...])
    lg = lax.dot_general(w_hi, h_hi, _NT, preferred_element_type=F32)
    lg = lg + lax.dot_general(w_lo, h_hi, _NT, preferred_element_type=F32)
    lg = lg + lax.dot_general(w_hi, h_lo, _NT, preferred_element_type=F32)
    lg_ref[...] = lg


def _out_proj_kernel(xp_ref, xs_ref, *rest, n_prompt_blocks):
    i = pl.program_id(0)

    @pl.when(i < n_prompt_blocks)
    def _():
        _out_proj_body(xp_ref, *rest)

    @pl.when(i >= n_prompt_blocks)
    def _():
        _out_proj_body(xs_ref, *rest)


def _out_proj(xp, xs, o_a, o_b, g_mix, w_out_bf16, g_ffn, w_router_t):
    tp, ts = xp.shape[0], xs.shape[0]
    t_all = tp + ts
    npb, nsb = tp // ROW_TILE, ts // ROW_TILE
    row = lambda i: (i, 0)
    const = lambda i: (0, 0)
    return pl.pallas_call(
        functools.partial(_out_proj_kernel, n_prompt_blocks=npb),
        grid=(npb + nsb,),
        in_specs=[
            pl.BlockSpec((ROW_TILE, D_MODEL), lambda i: (jnp.minimum(i, npb - 1), 0)),
            pl.BlockSpec((ROW_TILE, D_MODEL), lambda i: (jnp.maximum(i - npb, 0), 0)),
            pl.BlockSpec((ROW_TILE, QA_W), row),
            pl.BlockSpec((ROW_TILE, B_W), row),
            pl.BlockSpec((1, D_MODEL), const),
            pl.BlockSpec((D_MODEL, D_MODEL), const, pipeline_mode=pl.Buffered(1)),
            pl.BlockSpec((1, D_MODEL), const),
            pl.BlockSpec((ROUTER_ROWS, D_MODEL), const),
        ],
        out_specs=[
            pl.BlockSpec((ROW_TILE, D_MODEL), row),
            pl.BlockSpec((ROW_TILE, D_MODEL), row),
            pl.BlockSpec((ROUTER_ROWS, ROW_TILE), lambda i: (0, i)),
        ],
        out_shape=[
            jax.ShapeDtypeStruct((t_all, D_MODEL), F32),
            jax.ShapeDtypeStruct((t_all, D_MODEL), F32),
            jax.ShapeDtypeStruct((ROUTER_ROWS, t_all), F32),
        ],
        compiler_params=_cparams(("arbitrary",), VMEM_LIMIT),
        name="out_proj_router",
    )(xp, xs, o_a, o_b, g_mix.reshape(1, D_MODEL), w_out_bf16, g_ffn.reshape(1, D_MODEL), w_router_t)


def _route_kernel(l_ref, dest_ref, gate_ref, blk_ref, pad_ref, pref_scr, *, t_all, n_blk, n_slots):
    chunk = 256
    lanes = 128
    sub8 = lax.broadcasted_iota(I32, (EPG, t_all), 0)
    gl = jnp.where(sub8 < N_GROUPS, l_ref[N_EXPERTS:N_EXPERTS + EPG, :], -jnp.inf)
    gmax = jnp.max(gl, axis=0, keepdims=True)
    gidx = jnp.min(jnp.where(gl == gmax, sub8, EPG), axis=0, keepdims=True)
    p_grp = 1.0 / jnp.sum(jnp.exp(gl - gmax), axis=0, keepdims=True)

    e_sel = l_ref[(N_GROUPS - 1) * EPG:N_GROUPS * EPG, :]
    for g in range(N_GROUPS - 2, -1, -1):
        e_sel = jnp.where(gidx == g, l_ref[g * EPG:(g + 1) * EPG, :], e_sel)
    ee = jnp.exp(e_sel - jnp.max(e_sel, axis=0, keepdims=True))
    pe = ee * (1.0 / jnp.sum(ee, axis=0, keepdims=True))
    v1 = jnp.max(pe, axis=0, keepdims=True)
    i1 = jnp.min(jnp.where(pe == v1, sub8, EPG), axis=0, keepdims=True)
    pe2 = jnp.where(sub8 == i1, -1.0, pe)
    v2 = jnp.max(pe2, axis=0, keepdims=True)
    i2 = jnp.min(jnp.where(pe2 == v2, sub8, EPG), axis=0, keepdims=True)
    norm = p_grp * (1.0 / (v1 + v2))
    gate_ref[...] = jnp.where(sub8 == 0, v1 * norm, jnp.where(sub8 == 1, v2 * norm, 0.0))
    e1 = gidx * EPG + i1
    e2 = gidx * EPG + i2

    tri = jnp.where(lax.broadcasted_iota(I32, (chunk, chunk), 0) < lax.broadcasted_iota(I32, (chunk, chunk), 1),
                    1.0, 0.0).astype(BF16)
    sub32 = lax.broadcasted_iota(I32, (N_EXPERTS, chunk), 0)
    count = jnp.zeros((N_EXPERTS, 1), F32)
    for c in range(t_all // chunk):
        sl = slice(c * chunk, (c + 1) * chunk)
        member = jnp.where((sub32 == e1[:, sl]) | (sub32 == e2[:, sl]), 1.0, 0.0)
        pref_scr[:, sl] = jnp.dot(member.astype(BF16), tri, preferred_element_type=F32) + count
        count = count + jnp.sum(member, axis=1, keepdims=True)

    nblk_e = jnp.floor((count + (MOE_BLK - 1)) * (1.0 / MOE_BLK))
    nb_l = jnp.broadcast_to(nblk_e, (N_EXPERTS, lanes))
    hi16 = jnp.floor(nb_l * (1.0 / 16.0))
    lo16 = nb_l - 16.0 * hi16
    ltri = jnp.where(lax.broadcasted_iota(I32, (N_EXPERTS, N_EXPERTS), 1) < lax.broadcasted_iota(I32, (N_EXPERTS, N_EXPERTS), 0),
                     1.0, 0.0).astype(BF16)
    first_blk = (16.0 * jnp.dot(ltri, hi16.astype(BF16), preferred_element_type=F32)
                 + jnp.dot(ltri, lo16.astype(BF16), preferred_element_type=F32))
    end_blk = first_blk + nb_l
    pstart = first_blk[:, 0:1] * float(MOE_BLK)

    sub32t = lax.broadcasted_iota(I32, (N_EXPERTS, t_all), 0)
    slot = pref_scr[...] + pstart
    d1 = jnp.sum(jnp.where(sub32t == e1, slot, 0.0), axis=0, keepdims=True)
    d2 = jnp.sum(jnp.where(sub32t == e2, slot, 0.0), axis=0, keepdims=True)
    dest_ref[...] = jnp.where(sub8 == 0, d1, jnp.where(sub8 == 1, d2, 0.0)).astype(I32)

    blk_i = lax.broadcasted_iota(I32, (N_EXPERTS, n_blk), 1).astype(F32)
    owner = jnp.sum(jnp.where(end_blk[:, 0:1] <= blk_i, 1.0, 0.0), axis=0, keepdims=True)
    owner = jnp.minimum(owner, float(N_EXPERTS - 1))
    n_used = jnp.sum(nblk_e, axis=0, keepdims=True)
    row8 = lax.broadcasted_iota(I32, (8, n_blk), 0)
    blk_ref[...] = jnp.where(row8 == 0, owner, jnp.where(row8 == 1, n_used, 0.0)).astype(I32)
    pad_ref[0:N_EXPERTS, :] = (first_blk * float(MOE_BLK) + jnp.broadcast_to(count, (N_EXPERTS, lanes))).astype(I32)
    last = lax.broadcasted_iota(I32, (N_EXPERTS, lanes), 0) == N_EXPERTS - 1
    pad_ref[N_EXPERTS:, :] = jnp.where(last, float(n_slots), end_blk * float(MOE_BLK)).astype(I32)


def _route(logits_t, n_slots):
    t_all = logits_t.shape[1]
    n_blk = pl.cdiv(n_slots // MOE_BLK, 128) * 128
    vmem = pl.BlockSpec(memory_space=pltpu.VMEM)
    return pl.pallas_call(
        functools.partial(_route_kernel, t_all=t_all, n_blk=n_blk, n_slots=n_slots),
        in_specs=[vmem],
        out_specs=[vmem, vmem, vmem, vmem],
        out_shape=[
            jax.ShapeDtypeStruct((8, t_all), I32),
            jax.ShapeDtypeStruct((8, t_all), F32),
            jax.ShapeDtypeStruct((8, n_blk), I32),
            jax.ShapeDtypeStruct((2 * N_EXPERTS, 128), I32),
        ],
        scratch_shapes=[pltpu.VMEM((N_EXPERTS, t_all), F32)],
        compiler_params=pltpu.CompilerParams(vmem_limit_bytes=VMEM_LIMIT),
        name="route",
    )(logits_t)


def _row_copy(src_ref, src_row, dst_ref, dst_row, sem):
    return pltpu.make_async_copy(src_ref.at[pl.ds(src_row, 1), :], dst_ref.at[pl.ds(dst_row, 1), :], sem)


def _dispatch_kernel(d1_ref, d2_ref, padlo_ref, padhi_ref, h_ref, xs_ref, zero_ref, sem, zsem):
    i = pl.program_id(0)
    tm = h_ref.shape[0]
    tok0 = i * tm

    def issue(r, c):
        _row_copy(h_ref, r, xs_ref, d1_ref[tok0 + r], sem).start()
        _row_copy(h_ref, r, xs_ref, d2_ref[tok0 + r], sem).start()
        return c

    lax.fori_loop(0, tm, issue, 0)

    @pl.when(i == 0)
    def _():
        zero_ref[...] = jnp.zeros_like(zero_ref)

        def per_expert(e, c):
            def fill(s, c2):
                _row_copy(zero_ref, 0, xs_ref, s, zsem).start()
                return c2

            def drain(s, c2):
                _row_copy(zero_ref, 0, xs_ref, s, zsem).wait()
                return c2

            lax.fori_loop(padlo_ref[e], padhi_ref[e], fill, 0)
            lax.fori_loop(padlo_ref[e], padhi_ref[e], drain, 0)
            return c

        lax.fori_loop(0, N_EXPERTS, per_expert, 0)

    def drain_rows(r, c):
        _row_copy(h_ref, r, xs_ref, d1_ref[tok0 + r], sem).wait()
        _row_copy(h_ref, r, xs_ref, d2_ref[tok0 + r], sem).wait()
        return c

    lax.fori_loop(0, tm, drain_rows, 0)


def _dispatch(h2, d1, d2, pad_lo, pad_hi, n_slots):
    t_all = h2.shape[0]
    tm = DISPATCH_TILE
    return pl.pallas_call(
        _dispatch_kernel,
        grid_spec=pltpu.PrefetchScalarGridSpec(
            num_scalar_prefetch=4,
            grid=(t_all // tm,),
            in_specs=[pl.BlockSpec((tm, D_MODEL), lambda i, *_: (i, 0))],
            out_specs=pl.BlockSpec(memory_space=pl.ANY),
            scratch_shapes=[pltpu.VMEM((8, D_MODEL), F32), pltpu.SemaphoreType.DMA(()), pltpu.SemaphoreType.DMA(())],
        ),
        out_shape=jax.ShapeDtypeStruct((n_slots, D_MODEL), F32),
        compiler_params=pltpu.CompilerParams(dimension_semantics=("arbitrary",), has_side_effects=True),
        name="dispatch",
    )(d1, d2, pad_lo, pad_hi, h2)


def _expert_changed(b, blk_e_ref):
    prev = blk_e_ref[jnp.maximum(b - 1, 0)]
    return (b == 0) | (blk_e_ref[b] != prev)


def _moe_up_kernel(blk_e_ref, nused_ref, x_ref, wg_ref, wu_ref, hid_ref, wg_bf, wu_bf):
    b = pl.program_id(0)

    @pl.when(b < nused_ref[0])
    def _():
        @pl.when(_expert_changed(b, blk_e_ref))
        def _():
            wg_bf[...] = wg_ref[...].astype(BF16)
            wu_bf[...] = wu_ref[...].astype(BF16)

        x = x_ref[...].astype(BF16)
        nch = 512
        for j in range(D_EXPERT // nch):
            sl = slice(j * nch, (j + 1) * nch)
            gt = jnp.dot(x, wg_bf[:, sl], preferred_element_type=F32)
            up = jnp.dot(x, wu_bf[:, sl], preferred_element_type=F32)
            hid_ref[:, sl] = (gt * (1.0 / (1.0 + jnp.exp(-gt))) * up).astype(BF16)

    @pl.when(b >= nused_ref[0])
    def _():
        hid_ref[...] = jnp.zeros_like(hid_ref)


def _moe_down_kernel(blk_e_ref, nused_ref, hid_ref, wd_ref, y_ref, wd_bf):
    b = pl.program_id(0)

    @pl.when(b < nused_ref[0])
    def _():
        @pl.when(_expert_changed(b, blk_e_ref))
        def _():
            wd_bf[...] = wd_ref[...].astype(BF16)

        nch = 512
        for j in range(D_MODEL // nch):
            sl = slice(j * nch, (j + 1) * nch)
            y_ref[:, sl] = jnp.dot(hid_ref[...], wd_bf[:, sl], preferred_element_type=F32)

    @pl.when(b >= nused_ref[0])
    def _():
        y_ref[...] = jnp.zeros_like(y_ref)


def _blk(b, nused_ref):
    return jnp.minimum(b, nused_ref[0] - 1)


def _moe(xs, blk_e, n_used, w_gate, w_up, w_down):
    n_slots = xs.shape[0]
    n_blk = n_slots // MOE_BLK
    hid = pl.pallas_call(
        _moe_up_kernel,
        grid_spec=pltpu.PrefetchScalarGridSpec(
            num_scalar_prefetch=2,
            grid=(n_blk,),
            in_specs=[
                pl.BlockSpec((MOE_BLK, D_MODEL), lambda b, be, nu: (_blk(b, nu), 0)),
                pl.BlockSpec((None, D_MODEL, D_EXPERT), lambda b, be, nu: (be[_blk(b, nu)], 0, 0)),
                pl.BlockSpec((None, D_MODEL, D_EXPERT), lambda b, be, nu: (be[_blk(b, nu)], 0, 0)),
            ],
            out_specs=pl.BlockSpec((MOE_BLK, D_EXPERT), lambda b, be, nu: (b, 0)),
            scratch_shapes=[pltpu.VMEM((D_MODEL, D_EXPERT), BF16), pltpu.VMEM((D_MODEL, D_EXPERT), BF16)],
        ),
        out_shape=jax.ShapeDtypeStruct((n_slots, D_EXPERT), BF16),
        compiler_params=_cparams(("arbitrary",), VMEM_LIMIT),
        name="moe_gate_up",
    )(blk_e, n_used, xs, w_gate, w_up)
    return pl.pallas_call(
        _moe_down_kernel,
        grid_spec=pltpu.PrefetchScalarGridSpec(
            num_scalar_prefetch=2,
            grid=(n_blk,),
            in_specs=[
                pl.BlockSpec((MOE_BLK, D_EXPERT), lambda b, be, nu: (_blk(b, nu), 0)),
                pl.BlockSpec((None, D_EXPERT, D_MODEL), lambda b, be, nu: (be[_blk(b, nu)], 0, 0)),
            ],
            out_specs=pl.BlockSpec((MOE_BLK, D_MODEL), lambda b, be, nu: (b, 0)),
            scratch_shapes=[pltpu.VMEM((D_EXPERT, D_MODEL), BF16)],
        ),
        out_shape=jax.ShapeDtypeStruct((n_slots, D_MODEL), F32),
        compiler_params=_cparams(("arbitrary",), VMEM_LIMIT),
        name="moe_down",
    )(blk_e, n_used, hid, w_down)


def _combine_body(i, d1_ref, d2_ref, x1_ref, gate_ref, g_ref, ys_ref, out_ref, r1, r2, sem):
    tm = x1_ref.shape[0]
    tok0 = i * tm

    def issue(r, c):
        _row_copy(ys_ref, d1_ref[tok0 + r], r1, r, sem).start()
        _row_copy(ys_ref, d2_ref[tok0 + r], r2, r, sem).start()
        return c

    def drain(r, c):
        _row_copy(ys_ref, d1_ref[tok0 + r], r1, r, sem).wait()
        _row_copy(ys_ref, d2_ref[tok0 + r], r2, r, sem).wait()
        return c

    lax.fori_loop(0, tm, issue, 0)
    lax.fori_loop(0, tm, drain, 0)
    gt = gate_ref[...]
    x2 = x1_ref[...] + (r1[...] * gt[:, 0:1] + r2[...] * gt[:, 1:2])
    out_ref[...] = _rms_scale(x2) * g_ref[...]


def _combine_kernel(d1_ref, d2_ref, x1_ref, gate_ref, g_ref, ys_ref, outp_ref, outs_ref, r1, r2, sem, *, n_prompt_blocks):
    i = pl.program_id(0)

    @pl.when(i < n_prompt_blocks)
    def _():
        _combine_body(i, d1_ref, d2_ref, x1_ref, gate_ref, g_ref, ys_ref, outp_ref, r1, r2, sem)

    @pl.when(i >= n_prompt_blocks)
    def _():
        _combine_body(i, d1_ref, d2_ref, x1_ref, gate_ref, g_ref, ys_ref, outs_ref, r1, r2, sem)


def _combine(x1, gates_t, g_final, ys, d1, d2, tp, ts):
    tm = DISPATCH_TILE
    npb, nsb = tp // tm, ts // tm
    return pl.pallas_call(
        functools.partial(_combine_kernel, n_prompt_blocks=npb),
        grid_spec=pltpu.PrefetchScalarGridSpec(
            num_scalar_prefetch=2,
            grid=(npb + nsb,),
            in_specs=[
                pl.BlockSpec((tm, D_MODEL), lambda i, *_: (i, 0)),
                pl.BlockSpec((tm, 2), lambda i, *_: (i, 0)),
                pl.BlockSpec((1, D_MODEL), lambda i, *_: (0, 0)),
                pl.BlockSpec(memory_space=pl.ANY),
            ],
            out_specs=[
                pl.BlockSpec((tm, D_MODEL), lambda i, *_: (jnp.minimum(i, npb - 1), 0)),
                pl.BlockSpec((tm, D_MODEL), lambda i, *_: (jnp.maximum(i - npb, 0), 0)),
            ],
            scratch_shapes=[pltpu.VMEM((tm, D_MODEL), F32), pltpu.VMEM((tm, D_MODEL), F32), pltpu.SemaphoreType.DMA(())],
        ),
        out_shape=[jax.ShapeDtypeStruct((tp, D_MODEL), F32), jax.ShapeDtypeStruct((ts, D_MODEL), F32)],
        compiler_params=_cparams(("arbitrary",), VMEM_LIMIT),
        name="combine_final_norm",
    )(d1, d2, x1, gates_t, g_final.reshape(1, D_MODEL), ys)


def _layer(xp, xs, g_attn, w_in, g_mix, sink_a, rpb_b, w_out, g_ffn, w_rg, w_re, w_gate, w_up, w_down, g_final):
    tp, ts = xp.shape[0], xs.shape[0]
    t_all = tp + ts
    bounds = (0, tp, t_all)
    s0, s1, s2, s3, s4 = QA_W, QA_W + KVA_W, QA_W + 2 * KVA_W, QA_W + 2 * KVA_W + B_W, QA_W + 2 * KVA_W + 2 * B_W
    w_in_p = jnp.concatenate([w_in[:, :s0], w_in[:, s2:s3], w_in[:, s3:s4], w_in[:, s4:], w_in[:, s0:s1], w_in[:, s1:s2]],
                             axis=1).astype(BF16)
    w_router_t = jnp.concatenate([w_re.T, w_rg.T, jnp.zeros((ROUTER_ROWS - N_EXPERTS - N_GROUPS, D_MODEL), F32)], axis=0)

    proj = _in_proj(xp, xs, g_attn, w_in_p)
    o_a = _window_attention(proj, sink_a, bounds)
    o_b = _neighborhood_attention(proj, _bias_table(rpb_b), bounds)
    x1, h2, logits_t = _out_proj(xp, xs, o_a, o_b, g_mix, w_out.astype(BF16), g_ffn, w_router_t)

    n_slots = 2 * t_all + N_EXPERTS * MOE_BLK
    n_blk = n_slots // MOE_BLK
    dest, gate, blk, pad = _route(logits_t, n_slots)
    d1, d2 = dest[0], dest[1]
    xs_sorted = _dispatch(h2, d1, d2, pad[:N_EXPERTS, 0], pad[N_EXPERTS:, 0], n_slots)
    ys = _moe(xs_sorted, blk[0, :n_blk], blk[1, :1], w_gate, w_up, w_down)
    return _combine(x1, gate[:2].T, g_final, ys, d1, d2, tp, ts)


def kernel(x_prompt, x_sample, g_attn, w_in, g_mix, sink_a, rpb_b, w_out, g_ffn, w_router_group, w_router_expert,
           w_gate, w_up, w_down, g_final):
    assert x_prompt.shape[0] == 1 and x_sample.shape[0] == 1 and g_attn.shape[0] == 1
    yp, ys = _layer(x_prompt[0], x_sample[0], g_attn[0], w_in[0], g_mix[0], sink_a[0], rpb_b[0], w_out[0], g_ffn[0],
                    w_router_group[0], w_router_expert[0], w_gate[0], w_up[0], w_down[0], g_final)
    return yp[None], ys[None]
```

```python
import functools

import jax
import jax.numpy as jnp
from jax import lax
from jax.experimental import pallas as pl
from jax.experimental.pallas import tpu as pltpu

F32 = jnp.float32
BF16 = jnp.bfloat16
I32 = jnp.int32

D_MODEL = 2048
HEAD_DIM = 64
N_HEADS_A = 16
N_KV_A = 4
N_HEADS_B = 16
WIN = 128
GRID_W = 64
NA_ROWS = 8
NA_COLS = 16
N_GROUPS = 4
EPG = 8
N_EXPERTS = 32
D_EXPERT = 1024
EPS = 1e-6

QA_W = N_HEADS_A * HEAD_DIM
KVA_W = N_KV_A * HEAD_DIM
B_W = N_HEADS_B * HEAD_DIM
D_IN = QA_W + 2 * KVA_W + 3 * B_W
HEAD_GROUP_W = 256
Q_SCALE = HEAD_DIM ** -0.5

ROW_TILE = 512
NA_ROWS_PER_STEP = 8
MOE_BLK = 512
ROUTER_ROWS = 40
VMEM_LIMIT = 56 * 1024 * 1024
MOE_VMEM_LIMIT = 60 * 1024 * 1024


def _cparams(sem, vmem=None):
    return pltpu.CompilerParams(dimension_semantics=sem, vmem_limit_bytes=vmem)


def _rms_scale(x):
    return x * lax.rsqrt(jnp.mean(x * x, axis=-1, keepdims=True) + EPS)


def _in_proj_body(x_ref, g_ref, w_ref, o_ref):
    h = (_rms_scale(x_ref[...]) * g_ref[...]).astype(BF16)
    nch = 512
    for j in range(D_IN // nch):
        sl = slice(j * nch, (j + 1) * nch)
        o_ref[:, sl] = jnp.dot(h, w_ref[:, sl], preferred_element_type=F32).astype(BF16)


def _in_proj_kernel(xp_ref, xs_ref, g_ref, w_ref, o_ref, *, n_prompt_blocks):
    i = pl.program_id(0)

    @pl.when(i < n_prompt_blocks)
    def _():
        _in_proj_body(xp_ref, g_ref, w_ref, o_ref)

    @pl.when(i >= n_prompt_blocks)
    def _():
        _in_proj_body(xs_ref, g_ref, w_ref, o_ref)


def _in_proj(xp, xs, g_attn, w_in_bf16):
    tp, ts = xp.shape[0], xs.shape[0]
    npb, nsb = tp // ROW_TILE, ts // ROW_TILE
    return pl.pallas_call(
        functools.partial(_in_proj_kernel, n_prompt_blocks=npb),
        grid=(npb + nsb,),
        in_specs=[
            pl.BlockSpec((ROW_TILE, D_MODEL), lambda i: (jnp.minimum(i, npb - 1), 0)),
            pl.BlockSpec((ROW_TILE, D_MODEL), lambda i: (jnp.maximum(i - npb, 0), 0)),
            pl.BlockSpec((1, D_MODEL), lambda i: (0, 0)),
            pl.BlockSpec((D_MODEL, D_IN), lambda i: (0, 0), pipeline_mode=pl.Buffered(1)),
        ],
        out_specs=pl.BlockSpec((ROW_TILE, D_IN), lambda i: (i, 0)),
        out_shape=jax.ShapeDtypeStruct((tp + ts, D_IN), BF16),
        compiler_params=_cparams(("arbitrary",), VMEM_LIMIT),
        name="in_proj",
    )(xp, xs, g_attn.reshape(1, D_MODEL), w_in_bf16)


def _seq_bounds(pos, bounds):
    lo = jnp.int32(bounds[0])
    hi = jnp.int32(bounds[1])
    for k in range(1, len(bounds) - 1):
        inside = pos >= bounds[k]
        lo = jnp.where(inside, bounds[k], lo)
        hi = jnp.where(inside, bounds[k + 1], hi)
    return lo, hi


def _lane_head(width):
    return lax.broadcasted_iota(I32, (1, width), 1) // HEAD_DIM


def _stack_heads(x, lane_head):
    zero = jnp.zeros_like(x)
    return jnp.concatenate([jnp.where(lane_head == r, x, zero) for r in range(4)], axis=0)


def _diag_heads(o_all, m, lane_head):
    out = jnp.where(lane_head == 0, o_all[0:m], 0.0)
    for r in range(1, 4):
        out = out + jnp.where(lane_head == r, o_all[r * m:(r + 1) * m], 0.0)
    return out


_NT = (((1,), (1,)), ((), ()))


def _win_kernel(sink_ref, q_ref, kp_ref, km_ref, kn_ref, vp_ref, vm_ref, vn_ref, o_ref,
                kbuf, vbuf, bias_scr, *, bounds):
    i = pl.program_id(0)
    tq = q_ref.shape[0]
    n_sub = tq // WIN
    tok0 = i * tq
    lo, hi = _seq_bounds(tok0, bounds)
    lane_head = _lane_head(HEAD_GROUP_W)

    @pl.when(i == 0)
    def _():
        qi = lax.broadcasted_iota(I32, (WIN, 3 * WIN), 0)
        ki = lax.broadcasted_iota(I32, (WIN, 3 * WIN), 1)
        dist_i = jnp.abs(qi - ki + WIN)
        dist = dist_i.astype(F32)
        for h in range(N_HEADS_A):
            slope = 2.0 ** (-8.0 * (h + 1.0) / N_HEADS_A)
            base = jnp.where(dist_i <= WIN, -(slope * dist), -jnp.inf)
            bias_scr[0, h] = base
            bias_scr[1, h] = jnp.where(ki >= WIN, base, -jnp.inf)
            bias_scr[2, h] = jnp.where(ki < 2 * WIN, base, -jnp.inf)

    er = lax.broadcasted_iota(I32, (HEAD_GROUP_W, HEAD_GROUP_W), 0)
    ec = lax.broadcasted_iota(I32, (HEAD_GROUP_W, HEAD_GROUP_W), 1)
    for g in range(N_KV_A):
        spread = jnp.where((er // HEAD_DIM == g) & (er % HEAD_DIM == ec % HEAD_DIM), 1.0, 0.0).astype(BF16)
        for buf, (p_ref, m_ref, n_ref) in ((kbuf, (kp_ref, km_ref, kn_ref)), (vbuf, (vp_ref, vm_ref, vn_ref))):
            buf[g, 0:WIN, :] = jnp.dot(p_ref[...], spread, preferred_element_type=F32).astype(BF16)
            buf[g, WIN:WIN + tq, :] = jnp.dot(m_ref[...], spread, preferred_element_type=F32).astype(BF16)
            buf[g, WIN + tq:, :] = jnp.dot(n_ref[...], spread, preferred_element_type=F32).astype(BF16)

    for g in range(N_KV_A):
        for c in range(n_sub):
            r0 = c * WIN
            qg = q_ref[r0:r0 + WIN, g * HEAD_GROUP_W:(g + 1) * HEAD_GROUP_W] * Q_SCALE
            lhs = _stack_heads(qg, lane_head)
            kwin = kbuf[g, r0:r0 + 3 * WIN, :]
            vwin = vbuf[g, r0:r0 + 3 * WIN, :]
            s = lax.dot_general(lhs, kwin, _NT, preferred_element_type=F32)
            variant = 0
            if c == 0:
                variant = jnp.where(tok0 == lo, 1, 0)
            if c == n_sub - 1:
                variant = jnp.where(tok0 + tq == hi, 2, 0)
            probs = []
            for r in range(4):
                h = g * 4 + r
                sink = sink_ref[h]
                t = s[r * WIN:(r + 1) * WIN] + bias_scr[variant, h]
                m = jnp.maximum(jnp.max(t, axis=-1, keepdims=True), sink)
                p = jnp.exp(t - m)
                den = jnp.sum(p, axis=-1, keepdims=True) + jnp.exp(sink - m)
                probs.append((p * (1.0 / den)).astype(BF16))
            p_all = jnp.concatenate(probs, axis=0)
            o_all = jnp.dot(p_all, vwin, preferred_element_type=F32)
            o_ref[r0:r0 + WIN, g * HEAD_GROUP_W:(g + 1) * HEAD_GROUP_W] = _diag_heads(o_all, WIN, lane_head)


def _window_attention(proj, sink, bounds):
    t_all = proj.shape[0]
    tq = ROW_TILE
    nblk = t_all // tq
    halo_per_tile = tq // WIN
    n_halo = t_all // WIN
    k_col = (QA_W + 3 * B_W) // KVA_W
    v_col = k_col + 1

    def main_spec(col):
        return pl.BlockSpec((tq, KVA_W), lambda i: (i, col))

    def prev_spec(col):
        return pl.BlockSpec((WIN, KVA_W), lambda i: (jnp.maximum(i * halo_per_tile - 1, 0), col))

    def next_spec(col):
        return pl.BlockSpec((WIN, KVA_W), lambda i: (jnp.minimum((i + 1) * halo_per_tile, n_halo - 1), col))

    return pl.pallas_call(
        functools.partial(_win_kernel, bounds=bounds),
        grid=(nblk,),
        in_specs=[
            pl.BlockSpec(memory_space=pltpu.SMEM),
            pl.BlockSpec((tq, QA_W), lambda i: (i, 0)),
            prev_spec(k_col), main_spec(k_col), next_spec(k_col),
            prev_spec(v_col), main_spec(v_col), next_spec(v_col),
        ],
        out_specs=pl.BlockSpec((tq, QA_W), lambda i: (i, 0)),
        out_shape=jax.ShapeDtypeStruct((t_all, QA_W), F32),
        scratch_shapes=[pltpu.VMEM((N_KV_A, tq + 2 * WIN, KVA_W), BF16), pltpu.VMEM((N_KV_A, tq + 2 * WIN, KVA_W), BF16),
                        pltpu.VMEM((3, N_HEADS_A, WIN, 3 * WIN), F32)],
        compiler_params=_cparams(("arbitrary",), VMEM_LIMIT),
        name="window_attention",
    )(sink, proj, proj, proj, proj, proj, proj, proj)


def _bias_table_kernel(rpb_ref, o_ref):
    j0 = pl.program_id(0)
    g = pl.program_id(1)
    n_rel_rows = 2 * NA_ROWS - 1
    n_rel_cols = 2 * NA_COLS - 1
    qc = lax.broadcasted_iota(I32, (GRID_W, 2 * GRID_W), 0)
    lane = lax.broadcasted_iota(I32, (GRID_W, 2 * GRID_W), 1)
    kc = lane % GRID_W
    second = lane >= GRID_W
    rel = jnp.clip(kc - qc + NA_COLS - 1, 0, n_rel_cols - 1)
    col_start = jnp.clip(qc - NA_COLS // 2, 0, GRID_W - NA_COLS)
    col_valid = (kc >= col_start) & (kc < col_start + NA_COLS)
    for r in range(4):
        h = g * 4 + r
        for kp in range(NA_ROWS // 2):
            base_a = (h * n_rel_rows + j0 + 2 * kp) * n_rel_cols
            base_b = base_a + n_rel_cols

            tile = jnp.zeros((GRID_W, 2 * GRID_W), F32)
            for d in range(n_rel_cols):
                val = jnp.where(second, rpb_ref[base_b + d], rpb_ref[base_a + d])
                tile = jnp.where(rel == d, val, tile)
            o_ref[r * GRID_W:(r + 1) * GRID_W, kp * 2 * GRID_W:(kp + 1) * 2 * GRID_W] = jnp.where(col_valid, tile, -jnp.inf)


def _bias_table(rpb):
    n_j0 = NA_ROWS
    return pl.pallas_call(
        _bias_table_kernel,
        grid=(n_j0, 4),
        in_specs=[pl.BlockSpec(memory_space=pltpu.SMEM)],
        out_specs=pl.BlockSpec((None, None, 4 * GRID_W, NA_ROWS * GRID_W), lambda j, g: (j, g, 0, 0)),
        out_shape=jax.ShapeDtypeStruct((n_j0, 4, 4 * GRID_W, NA_ROWS * GRID_W), F32),
        compiler_params=_cparams(("arbitrary", "arbitrary")),
        name="na_bias_table",
    )(rpb.reshape(-1))


def _na_kernel(q_ref, kp_ref, km_ref, kn_ref, vp_ref, vm_ref, vn_ref, bt_ref, o_ref, kbuf, vbuf, *, row_bounds):
    i = pl.program_id(1)
    rows = NA_ROWS_PER_STEP
    tile = rows * GRID_W
    lane_head = _lane_head(HEAD_GROUP_W)
    for buf, (p_ref, m_ref, n_ref) in ((kbuf, (kp_ref, km_ref, kn_ref)), (vbuf, (vp_ref, vm_ref, vn_ref))):
        buf[0:tile, :] = p_ref[...]
        buf[tile:2 * tile, :] = m_ref[...]
        buf[2 * tile:3 * tile, :] = n_ref[...]
    row0 = i * rows
    lo, hi = _seq_bounds(row0, row_bounds)

    for jr in range(rows):
        r = row0 + jr
        rs = jnp.clip(r - NA_ROWS // 2, lo, hi - NA_ROWS)
        j0 = rs - r + NA_ROWS - 1
        off = pl.multiple_of((rs - row0 + rows) * GRID_W, GRID_W)
        q0 = jr * GRID_W
        q = q_ref[q0:q0 + GRID_W, :] * Q_SCALE
        lhs = _stack_heads(q, lane_head)
        kwin = kbuf[pl.ds(off, NA_ROWS * GRID_W), :]
        vwin = vbuf[pl.ds(off, NA_ROWS * GRID_W), :]
        s = lax.dot_general(lhs, kwin, _NT, preferred_element_type=F32) + bt_ref[j0]
        m = jnp.max(s, axis=-1, keepdims=True)
        p = jnp.exp(s - m)
        den = jnp.sum(p, axis=-1, keepdims=True)
        pb = (p * (1.0 / den)).astype(BF16)
        o_all = jnp.dot(pb, vwin, preferred_element_type=F32)
        o_ref[q0:q0 + GRID_W, :] = _diag_heads(o_all, GRID_W, lane_head)


def _neighborhood_attention(proj, bias_table, bounds):
    t_all = proj.shape[0]
    tile = NA_ROWS_PER_STEP * GRID_W
    nblk = t_all // tile
    row_bounds = tuple(b // GRID_W for b in bounds)
    q_col = QA_W // HEAD_GROUP_W
    k_col = (QA_W + B_W) // HEAD_GROUP_W
    v_col = (QA_W + 2 * B_W) // HEAD_GROUP_W

    def spec(col, shift):
        return pl.BlockSpec((tile, HEAD_GROUP_W), lambda g, i: (jnp.clip(i + shift, 0, nblk - 1), col + g))

    return pl.pallas_call(
        functools.partial(_na_kernel, row_bounds=row_bounds),
        grid=(4, nblk),
        in_specs=[
            spec(q_col, 0),
            spec(k_col, -1), spec(k_col, 0), spec(k_col, 1),
            spec(v_col, -1), spec(v_col, 0), spec(v_col, 1),
            pl.BlockSpec((NA_ROWS, None, 4 * GRID_W, NA_ROWS * GRID_W), lambda g, i: (0, g, 0, 0)),
        ],
        out_specs=pl.BlockSpec((tile, HEAD_GROUP_W), lambda g, i: (i, g)),
        out_shape=jax.ShapeDtypeStruct((t_all, B_W), F32),
        scratch_shapes=[pltpu.VMEM((3 * tile, HEAD_GROUP_W), BF16), pltpu.VMEM((3 * tile, HEAD_GROUP_W), BF16)],
        compiler_params=_cparams(("arbitrary", "arbitrary"), VMEM_LIMIT),
        name="neighborhood_attention",
    )(proj, proj, proj, proj, proj, proj, proj, bias_table)


def _split_bf16(x):
    hi = x.astype(BF16)
    lo = (x - hi.astype(F32)).astype(BF16)
    return hi, lo


def _out_proj_body(x_ref, oa_ref, ob_ref, gmix_ref, wout_ref, gffn_ref, wr_ref, x1_ref, h2_ref, lg_ref):
    ma = (_rms_scale(oa_ref[...]) * gmix_ref[:, :QA_W]).astype(BF16)
    mb = (_rms_scale(ob_ref[...]) * gmix_ref[:, QA_W:]).astype(BF16)
    y = jnp.dot(ma, wout_ref[:QA_W, :], preferred_element_type=F32)
    y = y + jnp.dot(mb, wout_ref[QA_W:, :], preferred_element_type=F32)
    x1 = x_ref[...] + y
    x1_ref[...] = x1
    h2 = _rms_scale(x1) * gffn_ref[...]
    h2_ref[...] = h2
    h_hi, h_lo = _split_bf16(h2)
    w_hi, w_lo = _split_bf16(wr_ref[...])
    lg = lax.dot_general(w_hi, h_hi, _NT, preferred_element_type=F32)
    lg = lg + lax.dot_general(w_lo, h_hi, _NT, preferred_element_type=F32)
    lg = lg + lax.dot_general(w_hi, h_lo, _NT, preferred_element_type=F32)
    lg_ref[...] = lg


def _out_proj_kernel(xp_ref, xs_ref, *rest, n_prompt_blocks):
    i = pl.program_id(0)

    @pl.when(i < n_prompt_blocks)
    def _():
        _out_proj_body(xp_ref, *rest)

    @pl.when(i >= n_prompt_blocks)
    def _():
        _out_proj_body(xs_ref, *rest)


def _out_proj(xp, xs, o_a, o_b, g_mix, w_out_bf16, g_ffn, w_router_t):
    tp, ts = xp.shape[0], xs.shape[0]
    t_all = tp + ts
    npb, nsb = tp // ROW_TILE, ts // ROW_TILE
    row = lambda i: (i, 0)
    const = lambda i: (0, 0)
    return pl.pallas_call(
        functools.partial(_out_proj_kernel, n_prompt_blocks=npb),
        grid=(npb + nsb,),
        in_specs=[
            pl.BlockSpec((ROW_TILE, D_MODEL), lambda i: (jnp.minimum(i, npb - 1), 0)),
            pl.BlockSpec((ROW_TILE, D_MODEL), lambda i: (jnp.maximum(i - npb, 0), 0)),
            pl.BlockSpec((ROW_TILE, QA_W), row),
            pl.BlockSpec((ROW_TILE, B_W), row),
            pl.BlockSpec((1, D_MODEL), const),
            pl.BlockSpec((D_MODEL, D_MODEL), const, pipeline_mode=pl.Buffered(1)),
            pl.BlockSpec((1, D_MODEL), const),
            pl.BlockSpec((ROUTER_ROWS, D_MODEL), const),
        ],
        out_specs=[
            pl.BlockSpec((ROW_TILE, D_MODEL), row),
            pl.BlockSpec((ROW_TILE, D_MODEL), row),
            pl.BlockSpec((ROUTER_ROWS, ROW_TILE), lambda i: (0, i)),
        ],
        out_shape=[
            jax.ShapeDtypeStruct((t_all, D_MODEL), F32),
            jax.ShapeDtypeStruct((t_all, D_MODEL), F32),
            jax.ShapeDtypeStruct((ROUTER_ROWS, t_all), F32),
        ],
        compiler_params=_cparams(("arbitrary",), VMEM_LIMIT),
        name="out_proj_router",
    )(xp, xs, o_a, o_b, g_mix.reshape(1, D_MODEL), w_out_bf16, g_ffn.reshape(1, D_MODEL), w_router_t)


def _route_kernel(l_ref, dest_ref, gate_ref, blk_ref, pad_ref, pref_scr, *, t_all, n_blk, n_slots):
    chunk = 256
    lanes = 128
    sub8 = lax.broadcasted_iota(I32, (EPG, t_all), 0)
    gl = jnp.where(sub8 < N_GROUPS, l_ref[N_EXPERTS:N_EXPERTS + EPG, :], -jnp.inf)
    gmax = jnp.max(gl, axis=0, keepdims=True)
    gidx = jnp.min(jnp.where(gl == gmax, sub8, EPG), axis=0, keepdims=True)
    p_grp = 1.0 / jnp.sum(jnp.exp(gl - gmax), axis=0, keepdims=True)

    e_sel = l_ref[(N_GROUPS - 1) * EPG:N_GROUPS * EPG, :]
    for g in range(N_GROUPS - 2, -1, -1):
        e_sel = jnp.where(gidx == g, l_ref[g * EPG:(g + 1) * EPG, :], e_sel)
    ee = jnp.exp(e_sel - jnp.max(e_sel, axis=0, keepdims=True))
    pe = ee * (1.0 / jnp.sum(ee, axis=0, keepdims=True))
    v1 = jnp.max(pe, axis=0, keepdims=True)
    i1 = jnp.min(jnp.where(pe == v1, sub8, EPG), axis=0, keepdims=True)
    pe2 = jnp.where(sub8 == i1, -1.0, pe)
    v2 = jnp.max(pe2, axis=0, keepdims=True)
    i2 = jnp.min(jnp.where(pe2 == v2, sub8, EPG), axis=0, keepdims=True)
    norm = p_grp * (1.0 / (v1 + v2))
    gate_ref[...] = jnp.where(sub8 == 0, v1 * norm, jnp.where(sub8 == 1, v2 * norm, 0.0))
    e1 = gidx * EPG + i1
    e2 = gidx * EPG + i2

    tri = jnp.where(lax.broadcasted_iota(I32, (chunk, chunk), 0) < lax.broadcasted_iota(I32, (chunk, chunk), 1),
                    1.0, 0.0).astype(BF16)
    sub32 = lax.broadcasted_iota(I32, (N_EXPERTS, chunk), 0)
    count = jnp.zeros((N_EXPERTS, 1), F32)
    for c in range(t_all // chunk):
        sl = slice(c * chunk, (c + 1) * chunk)
        member = jnp.where((sub32 == e1[:, sl]) | (sub32 == e2[:, sl]), 1.0, 0.0)
        pref_scr[:, sl] = jnp.dot(member.astype(BF16), tri, preferred_element_type=F32) + count
        count = count + jnp.sum(member, axis=1, keepdims=True)

    nblk_e = jnp.floor((count + (MOE_BLK - 1)) * (1.0 / MOE_BLK))
    nb_l = jnp.broadcast_to(nblk_e, (N_EXPERTS, lanes))
    hi16 = jnp.floor(nb_l * (1.0 / 16.0))
    lo16 = nb_l - 16.0 * hi16
    ltri = jnp.where(lax.broadcasted_iota(I32, (N_EXPERTS, N_EXPERTS), 1) < lax.broadcasted_iota(I32, (N_EXPERTS, N_EXPERTS), 0),
                     1.0, 0.0).astype(BF16)
    first_blk = (16.0 * jnp.dot(ltri, hi16.astype(BF16), preferred_element_type=F32)
                 + jnp.dot(ltri, lo16.astype(BF16), preferred_element_type=F32))
    end_blk = first_blk + nb_l
    pstart = first_blk[:, 0:1] * float(MOE_BLK)

    sub32t = lax.broadcasted_iota(I32, (N_EXPERTS, t_all), 0)
    slot = pref_scr[...] + pstart
    d1 = jnp.sum(jnp.where(sub32t == e1, slot, 0.0), axis=0, keepdims=True)
    d2 = jnp.sum(jnp.where(sub32t == e2, slot, 0.0), axis=0, keepdims=True)
    dest_ref[...] = jnp.where(sub8 == 0, d1, jnp.where(sub8 == 1, d2, 0.0)).astype(I32)

    blk_i = lax.broadcasted_iota(I32, (N_EXPERTS, n_blk), 1).astype(F32)
    owner = jnp.sum(jnp.where(end_blk[:, 0:1] <= blk_i, 1.0, 0.0), axis=0, keepdims=True)
    owner = jnp.minimum(owner, float(N_EXPERTS - 1))
    n_used = jnp.sum(nblk_e, axis=0, keepdims=True)
    row8 = lax.broadcasted_iota(I32, (8, n_blk), 0)
    blk_ref[...] = jnp.where(row8 == 0, owner, jnp.where(row8 == 1, n_used, 0.0)).astype(I32)
    pad_ref[0:N_EXPERTS, :] = (first_blk * float(MOE_BLK) + jnp.broadcast_to(count, (N_EXPERTS, lanes))).astype(I32)
    last = lax.broadcasted_iota(I32, (N_EXPERTS, lanes), 0) == N_EXPERTS - 1
    pad_ref[N_EXPERTS:, :] = jnp.where(last, float(n_slots), end_blk * float(MOE_BLK)).astype(I32)


def _route(logits_t, n_slots):
    t_all = logits_t.shape[1]
    n_blk = pl.cdiv(n_slots // MOE_BLK, 128) * 128
    vmem = pl.BlockSpec(memory_space=pltpu.VMEM)
    return pl.pallas_call(
        functools.partial(_route_kernel, t_all=t_all, n_blk=n_blk, n_slots=n_slots),
        in_specs=[vmem],
        out_specs=[vmem, vmem, vmem, vmem],
        out_shape=[
            jax.ShapeDtypeStruct((8, t_all), I32),
            jax.ShapeDtypeStruct((8, t_all), F32),
            jax.ShapeDtypeStruct((8, n_blk), I32),
            jax.ShapeDtypeStruct((2 * N_EXPERTS, 128), I32),
        ],
        scratch_shapes=[pltpu.VMEM((N_EXPERTS, t_all), F32)],
        compiler_params=pltpu.CompilerParams(vmem_limit_bytes=VMEM_LIMIT),
        name="route",
    )(logits_t)


def _slot_map_kernel(d1_ref, d2_ref, padlo_ref, padhi_ref, slot_ref, *, t_all, t_pad):
    def per_expert(e, c):
        def fill(s, c2):
            slot_ref[s] = t_all + jnp.bitwise_and(s, 2 * MOE_BLK - 1)
            return c2

        lax.fori_loop(padlo_ref[e], padhi_ref[e], fill, 0)
        return c

    lax.fori_loop(0, N_EXPERTS, per_expert, 0)
    unroll = 8

    def body(j, c):
        for u in range(unroll):
            t = j * unroll + u
            slot_ref[d1_ref[t]] = t
            slot_ref[d2_ref[t]] = t_pad + t
        return c

    lax.fori_loop(0, t_all // unroll, body, 0)


def _slot_map(d1, d2, pad_lo, pad_hi, n_slots, t_pad):
    smem = pl.BlockSpec(memory_space=pltpu.SMEM)
    return pl.pallas_call(
        functools.partial(_slot_map_kernel, t_all=d1.shape[0], t_pad=t_pad),
        in_specs=[smem, smem, smem, smem],
        out_specs=smem,
        out_shape=jax.ShapeDtypeStruct((n_slots,), I32),
        name="slot_map",
    )(d1, d2, pad_lo, pad_hi)


def _expert_changed(b, blk_e_ref):
    prev = blk_e_ref[jnp.maximum(b - 1, 0)]
    return (b == 0) | (blk_e_ref[b] != prev)


def _gather_start(slot_ref, h_hbm, xbuf, sem, blk, buf, r, t_all, t_pad):
    row = slot_ref[blk * MOE_BLK + r]
    tok = jnp.minimum(jnp.where(row >= t_pad, row - t_pad, row), t_all - 1)
    pltpu.make_async_copy(h_hbm.at[pl.ds(tok, 1), :], xbuf.at[buf, pl.ds(r, 1), :], sem.at[buf]).start()


def _gather_wait(h_hbm, xbuf, sem, buf):
    for r in range(MOE_BLK):
        pltpu.make_async_copy(h_hbm.at[pl.ds(0, 1), :], xbuf.at[buf, pl.ds(r, 1), :], sem.at[buf]).wait()


def _moe_up_kernel(blk_e_ref, nused_ref, slot_ref, h_hbm, wg_ref, wu_ref, hid_ref, xbuf, sem, wg_bf, wu_bf,
                   *, t_all, t_pad):
    b = pl.program_id(0)
    nused = nused_ref[0]
    cur = lax.rem(b, 2)
    n_chunks = 4
    rows_per_chunk = MOE_BLK // n_chunks
    nch = D_EXPERT // n_chunks

    @pl.when(b == 0)
    def _():
        for r in range(MOE_BLK):
            _gather_start(slot_ref, h_hbm, xbuf, sem, 0, 0, r, t_all, t_pad)

    @pl.when(b < nused)
    def _():
        _gather_wait(h_hbm, xbuf, sem, cur)

        @pl.when(_expert_changed(b, blk_e_ref))
        def _():
            wg_bf[...] = wg_ref[...].astype(BF16)
            wu_bf[...] = wu_ref[...].astype(BF16)

        nxt = jnp.minimum(b + 1, nused - 1)
        x = xbuf[cur].astype(BF16)
        for j in range(n_chunks):
            for r in range(j * rows_per_chunk, (j + 1) * rows_per_chunk):
                _gather_start(slot_ref, h_hbm, xbuf, sem, nxt, 1 - cur, r, t_all, t_pad)
            sl = slice(j * nch, (j + 1) * nch)
            gt = jnp.dot(x, wg_bf[:, sl], preferred_element_type=F32)
            up = jnp.dot(x, wu_bf[:, sl], preferred_element_type=F32)
            hid_ref[:, sl] = (gt * (1.0 / (1.0 + jnp.exp(-gt))) * up).astype(BF16)

        @pl.when(b == nused - 1)
        def _():
            _gather_wait(h_hbm, xbuf, sem, 1 - cur)

    @pl.when(b >= nused)
    def _():
        hid_ref[...] = jnp.zeros_like(hid_ref)


def _scatter_start(slot_ref, ybuf, y_hbm, sem, blk, buf, r):
    row = slot_ref[blk * MOE_BLK + r]
    pltpu.make_async_copy(ybuf.at[buf, pl.ds(r, 1), :], y_hbm.at[pl.ds(row, 1), :], sem.at[buf]).start()


def _scatter_wait(ybuf, y_hbm, sem, buf):
    for r in range(MOE_BLK):
        pltpu.make_async_copy(ybuf.at[buf, pl.ds(r, 1), :], y_hbm.at[pl.ds(0, 1), :], sem.at[buf]).wait()


def _moe_down_kernel(blk_e_ref, nused_ref, slot_ref, hid_ref, wd_ref, y_hbm, ybuf, sem, wd_bf, *, t_all, t_pad):
    b = pl.program_id(0)
    nused = nused_ref[0]
    cur = lax.rem(b, 2)
    n_chunks = 4
    rows_per_chunk = MOE_BLK // n_chunks
    nch = D_MODEL // n_chunks

    @pl.when(b == 0)
    def _():
        ybuf[0] = jnp.zeros((MOE_BLK, D_MODEL), F32)
        for half in range(2):
            for part in range(2):
                row0 = half * t_pad + t_all + part * MOE_BLK
                cp = pltpu.make_async_copy(ybuf.at[0], y_hbm.at[pl.ds(row0, MOE_BLK), :], sem.at[0])
                cp.start()
                cp.wait()

    @pl.when((b >= 2) & (b <= nused))
    def _():
        _scatter_wait(ybuf, y_hbm, sem, cur)

    def compute(scatter_previous):
        h = hid_ref[...]
        for j in range(n_chunks):
            if scatter_previous:
                for r in range(j * rows_per_chunk, (j + 1) * rows_per_chunk):
                    _scatter_start(slot_ref, ybuf, y_hbm, sem, b - 1, 1 - cur, r)
            sl = slice(j * nch, (j + 1) * nch)
            ybuf[cur, :, sl] = jnp.dot(h, wd_bf[:, sl], preferred_element_type=F32)

    @pl.when(b == 0)
    def _():
        wd_bf[...] = wd_ref[...].astype(BF16)
        compute(False)

    @pl.when((b >= 1) & (b < nused))
    def _():
        @pl.when(_expert_changed(b, blk_e_ref))
        def _():
            wd_bf[...] = wd_ref[...].astype(BF16)

        compute(True)

    @pl.when(b == nused)
    def _():
        for r in range(MOE_BLK):
            _scatter_start(slot_ref, ybuf, y_hbm, sem, b - 1, 1 - cur, r)
        _scatter_wait(ybuf, y_hbm, sem, 1 - cur)


def _blk(b, nused_ref):
    return jnp.minimum(b, nused_ref[0] - 1)


def _moe(h2, slot, blk_e, n_used, w_gate, w_up, w_down, t_pad):
    t_all = h2.shape[0]
    n_slots = slot.shape[0]
    n_blk = n_slots // MOE_BLK
    hid = pl.pallas_call(
        functools.partial(_moe_up_kernel, t_all=t_all, t_pad=t_pad),
        grid_spec=pltpu.PrefetchScalarGridSpec(
            num_scalar_prefetch=3,
            grid=(n_blk,),
            in_specs=[
                pl.BlockSpec(memory_space=pl.ANY),
                pl.BlockSpec((None, D_MODEL, D_EXPERT), lambda b, be, nu, sl: (be[_blk(b, nu)], 0, 0)),
                pl.BlockSpec((None, D_MODEL, D_EXPERT), lambda b, be, nu, sl: (be[_blk(b, nu)], 0, 0)),
            ],
            out_specs=pl.BlockSpec((MOE_BLK, D_EXPERT), lambda b, be, nu, sl: (b, 0)),
            scratch_shapes=[pltpu.VMEM((2, MOE_BLK, D_MODEL), F32), pltpu.SemaphoreType.DMA((2,)),
                            pltpu.VMEM((D_MODEL, D_EXPERT), BF16), pltpu.VMEM((D_MODEL, D_EXPERT), BF16)],
        ),
        out_shape=jax.ShapeDtypeStruct((n_slots, D_EXPERT), BF16),
        compiler_params=_cparams(("arbitrary",), MOE_VMEM_LIMIT),
        name="moe_gate_up",
    )(blk_e, n_used, slot, h2, w_gate, w_up)
    return pl.pallas_call(
        functools.partial(_moe_down_kernel, t_all=t_all, t_pad=t_pad),
        grid_spec=pltpu.PrefetchScalarGridSpec(
            num_scalar_prefetch=3,
            grid=(n_blk,),
            in_specs=[
                pl.BlockSpec((MOE_BLK, D_EXPERT), lambda b, be, nu, sl: (_blk(b, nu), 0)),
                pl.BlockSpec((None, D_EXPERT, D_MODEL), lambda b, be, nu, sl: (be[_blk(b, nu)], 0, 0)),
            ],
            out_specs=pl.BlockSpec(memory_space=pl.ANY),
            scratch_shapes=[pltpu.VMEM((2, MOE_BLK, D_MODEL), F32), pltpu.SemaphoreType.DMA((2,)),
                            pltpu.VMEM((D_EXPERT, D_MODEL), BF16)],
        ),
        out_shape=jax.ShapeDtypeStruct((2 * t_pad, D_MODEL), F32),
        compiler_params=_cparams(("arbitrary",), VMEM_LIMIT),
        name="moe_down",
    )(blk_e, n_used, slot, hid, w_down)


def _combine_body(x1_ref, ya_ref, yb_ref, gate_ref, g_ref, out_ref):
    gt = gate_ref[...]
    x2 = x1_ref[...] + (ya_ref[...] * gt[:, 0:1] + yb_ref[...] * gt[:, 1:2])
    out_ref[...] = _rms_scale(x2) * g_ref[...]


def _combine_kernel(x1_ref, ya_ref, yb_ref, gate_ref, g_ref, outp_ref, outs_ref, *, n_prompt_blocks):
    i = pl.program_id(0)

    @pl.when(i < n_prompt_blocks)
    def _():
        _combine_body(x1_ref, ya_ref, yb_ref, gate_ref, g_ref, outp_ref)

    @pl.when(i >= n_prompt_blocks)
    def _():
        _combine_body(x1_ref, ya_ref, yb_ref, gate_ref, g_ref, outs_ref)


def _combine(x1, gates_t, g_final, y2, tp, ts, t_pad):
    tm = ROW_TILE
    npb, nsb = tp // tm, ts // tm
    second = t_pad // tm
    return pl.pallas_call(
        functools.partial(_combine_kernel, n_prompt_blocks=npb),
        grid=(npb + nsb,),
        in_specs=[
            pl.BlockSpec((tm, D_MODEL), lambda i: (i, 0)),
            pl.BlockSpec((tm, D_MODEL), lambda i: (i, 0)),
            pl.BlockSpec((tm, D_MODEL), lambda i: (second + i, 0)),
            pl.BlockSpec((tm, 2), lambda i: (i, 0)),
            pl.BlockSpec((1, D_MODEL), lambda i: (0, 0)),
        ],
        out_specs=[
            pl.BlockSpec((tm, D_MODEL), lambda i: (jnp.minimum(i, npb - 1), 0)),
            pl.BlockSpec((tm, D_MODEL), lambda i: (jnp.maximum(i - npb, 0), 0)),
        ],
        out_shape=[jax.ShapeDtypeStruct((tp, D_MODEL), F32), jax.ShapeDtypeStruct((ts, D_MODEL), F32)],
        compiler_params=_cparams(("arbitrary",), VMEM_LIMIT),
        name="combine_final_norm",
    )(x1, y2, y2, gates_t, g_final.reshape(1, D_MODEL))


def _layer(xp, xs, g_attn, w_in, g_mix, sink_a, rpb_b, w_out, g_ffn, w_rg, w_re, w_gate, w_up, w_down, g_final):
    tp, ts = xp.shape[0], xs.shape[0]
    t_all = tp + ts
    bounds = (0, tp, t_all)
    s0, s1, s2, s3, s4 = QA_W, QA_W + KVA_W, QA_W + 2 * KVA_W, QA_W + 2 * KVA_W + B_W, QA_W + 2 * KVA_W + 2 * B_W
    w_in_p = jnp.concatenate([w_in[:, :s0], w_in[:, s2:s3], w_in[:, s3:s4], w_in[:, s4:], w_in[:, s0:s1], w_in[:, s1:s2]],
                             axis=1).astype(BF16)
    w_router_t = jnp.concatenate([w_re.T, w_rg.T, jnp.zeros((ROUTER_ROWS - N_EXPERTS - N_GROUPS, D_MODEL), F32)], axis=0)

    proj = _in_proj(xp, xs, g_attn, w_in_p)
    o_a = _window_attention(proj, sink_a, bounds)
    o_b = _neighborhood_attention(proj, _bias_table(rpb_b), bounds)
    x1, h2, logits_t = _out_proj(xp, xs, o_a, o_b, g_mix, w_out.astype(BF16), g_ffn, w_router_t)

    n_slots = 2 * t_all + N_EXPERTS * MOE_BLK
    n_blk = n_slots // MOE_BLK
    t_pad = t_all + 2 * MOE_BLK
    dest, gate, blk, pad = _route(logits_t, n_slots)
    slot = _slot_map(dest[0], dest[1], pad[:N_EXPERTS, 0], pad[N_EXPERTS:, 0], n_slots, t_pad)
    y2 = _moe(h2, slot, blk[0, :n_blk], blk[1, :1], w_gate, w_up, w_down, t_pad)
    return _combine(x1, gate[:2].T, g_final, y2, tp, ts, t_pad)


def kernel(x_prompt, x_sample, g_attn, w_in, g_mix, sink_a, rpb_b, w_out, g_ffn, w_router_group, w_router_expert,
           w_gate, w_up, w_down, g_final):
    assert x_prompt.shape[0] == 1 and x_sample.shape[0] == 1 and g_attn.shape[0] == 1
    yp, ys = _layer(x_prompt[0], x_sample[0], g_attn[0], w_in[0], g_mix[0], sink_a[0], rpb_b[0], w_out[0], g_ffn[0],
                    w_router_group[0], w_router_expert[0], w_gate[0], w_up[0], w_down[0], g_final)
    return yp[None], ys[None]
```

```python
import functools

import jax
import jax.numpy as jnp
from jax import lax
from jax.experimental import pallas as pl
from jax.experimental.pallas import tpu as pltpu

F32 = jnp.float32
BF16 = jnp.bfloat16
I32 = jnp.int32

D_MODEL = 2048
HEAD_DIM = 64
N_HEADS_A = 16
N_KV_A = 4
N_HEADS_B = 16
WIN = 128
GRID_W = 64
NA_ROWS = 8
NA_COLS = 16
N_GROUPS = 4
EPG = 8
N_EXPERTS = 32
D_EXPERT = 1024
EPS = 1e-6

QA_W = N_HEADS_A * HEAD_DIM
KVA_W = N_KV_A * HEAD_DIM
B_W = N_HEADS_B * HEAD_DIM
D_IN = QA_W + 2 * KVA_W + 3 * B_W
HEAD_GROUP_W = 256
Q_SCALE = HEAD_DIM ** -0.5

ROW_TILE = 512
NA_ROWS_PER_STEP = 8
MOE_BLK = 512
ROUTER_ROWS = 40
VMEM_LIMIT = 56 * 1024 * 1024
MOE_VMEM_LIMIT = 60 * 1024 * 1024


def _cparams(sem, vmem=None):
    return pltpu.CompilerParams(dimension_semantics=sem, vmem_limit_bytes=vmem)


def _rms_scale(x):
    return x * lax.rsqrt(jnp.mean(x * x, axis=-1, keepdims=True) + EPS)


def _in_proj_body(x_ref, g_ref, w_ref, o_ref):
    h = (_rms_scale(x_ref[...]) * g_ref[...]).astype(BF16)
    nch = 512
    for j in range(D_IN // nch):
        sl = slice(j * nch, (j + 1) * nch)
        o_ref[:, sl] = jnp.dot(h, w_ref[:, sl], preferred_element_type=F32).astype(BF16)


def _in_proj_kernel(xp_ref, xs_ref, g_ref, w_ref, o_ref, *, n_prompt_blocks):
    i = pl.program_id(0)

    @pl.when(i < n_prompt_blocks)
    def _():
        _in_proj_body(xp_ref, g_ref, w_ref, o_ref)

    @pl.when(i >= n_prompt_blocks)
    def _():
        _in_proj_body(xs_ref, g_ref, w_ref, o_ref)


def _in_proj(xp, xs, g_attn, w_in_bf16):
    tp, ts = xp.shape[0], xs.shape[0]
    npb, nsb = tp // ROW_TILE, ts // ROW_TILE
    return pl.pallas_call(
        functools.partial(_in_proj_kernel, n_prompt_blocks=npb),
        grid=(npb + nsb,),
        in_specs=[
            pl.BlockSpec((ROW_TILE, D_MODEL), lambda i: (jnp.minimum(i, npb - 1), 0)),
            pl.BlockSpec((ROW_TILE, D_MODEL), lambda i: (jnp.maximum(i - npb, 0), 0)),
            pl.BlockSpec((1, D_MODEL), lambda i: (0, 0)),
            pl.BlockSpec((D_MODEL, D_IN), lambda i: (0, 0), pipeline_mode=pl.Buffered(1)),
        ],
        out_specs=pl.BlockSpec((ROW_TILE, D_IN), lambda i: (i, 0)),
        out_shape=jax.ShapeDtypeStruct((tp + ts, D_IN), BF16),
        compiler_params=_cparams(("arbitrary",), VMEM_LIMIT),
        name="in_proj",
    )(xp, xs, g_attn.reshape(1, D_MODEL), w_in_bf16)


def _seq_bounds(pos, bounds):
    lo = jnp.int32(bounds[0])
    hi = jnp.int32(bounds[1])
    for k in range(1, len(bounds) - 1):
        inside = pos >= bounds[k]
        lo = jnp.where(inside, bounds[k], lo)
        hi = jnp.where(inside, bounds[k + 1], hi)
    return lo, hi


def _lane_head(width):
    return lax.broadcasted_iota(I32, (1, width), 1) // HEAD_DIM


def _stack_heads(x, lane_head):
    zero = jnp.zeros_like(x)
    return jnp.concatenate([jnp.where(lane_head == r, x, zero) for r in range(4)], axis=0)


def _diag_heads(o_all, m, lane_head):
    out = jnp.where(lane_head == 0, o_all[0:m], 0.0)
    for r in range(1, 4):
        out = out + jnp.where(lane_head == r, o_all[r * m:(r + 1) * m], 0.0)
    return out


_NT = (((1,), (1,)), ((), ()))


def _win_kernel(sink_ref, q_ref, kp_ref, km_ref, kn_ref, vp_ref, vm_ref, vn_ref, o_ref,
                kbuf, vbuf, bias_scr, *, bounds):
    i = pl.program_id(0)
    tq = q_ref.shape[0]
    n_sub = tq // WIN
    tok0 = i * tq
    lo, hi = _seq_bounds(tok0, bounds)
    lane_head = _lane_head(HEAD_GROUP_W)

    @pl.when(i == 0)
    def _():
        qi = lax.broadcasted_iota(I32, (WIN, 3 * WIN), 0)
        ki = lax.broadcasted_iota(I32, (WIN, 3 * WIN), 1)
        dist_i = jnp.abs(qi - ki + WIN)
        dist = dist_i.astype(F32)
        for h in range(N_HEADS_A):
            slope = 2.0 ** (-8.0 * (h + 1.0) / N_HEADS_A)
            base = jnp.where(dist_i <= WIN, -(slope * dist), -jnp.inf)
            bias_scr[0, h] = base
            bias_scr[1, h] = jnp.where(ki >= WIN, base, -jnp.inf)
            bias_scr[2, h] = jnp.where(ki < 2 * WIN, base, -jnp.inf)

    er = lax.broadcasted_iota(I32, (HEAD_GROUP_W, HEAD_GROUP_W), 0)
    ec = lax.broadcasted_iota(I32, (HEAD_GROUP_W, HEAD_GROUP_W), 1)
    for g in range(N_KV_A):
        spread = jnp.where((er // HEAD_DIM == g) & (er % HEAD_DIM == ec % HEAD_DIM), 1.0, 0.0).astype(BF16)
        for buf, (p_ref, m_ref, n_ref) in ((kbuf, (kp_ref, km_ref, kn_ref)), (vbuf, (vp_ref, vm_ref, vn_ref))):
            buf[g, 0:WIN, :] = jnp.dot(p_ref[...], spread, preferred_element_type=F32).astype(BF16)
            buf[g, WIN:WIN + tq, :] = jnp.dot(m_ref[...], spread, preferred_element_type=F32).astype(BF16)
            buf[g, WIN + tq:, :] = jnp.dot(n_ref[...], spread, preferred_element_type=F32).astype(BF16)

    for g in range(N_KV_A):
        for c in range(n_sub):
            r0 = c * WIN
            qg = q_ref[r0:r0 + WIN, g * HEAD_GROUP_W:(g + 1) * HEAD_GROUP_W] * Q_SCALE
            lhs = _stack_heads(qg, lane_head)
            kwin = kbuf[g, r0:r0 + 3 * WIN, :]
            vwin = vbuf[g, r0:r0 + 3 * WIN, :]
            s = lax.dot_general(lhs, kwin, _NT, preferred_element_type=F32)
            variant = 0
            if c == 0:
                variant = jnp.where(tok0 == lo, 1, 0)
            if c == n_sub - 1:
                variant = jnp.where(tok0 + tq == hi, 2, 0)
            probs = []
            for r in range(4):
                h = g * 4 + r
                sink = sink_ref[h]
                t = s[r * WIN:(r + 1) * WIN] + bias_scr[variant, h]
                m = jnp.maximum(jnp.max(t, axis=-1, keepdims=True), sink)
                p = jnp.exp(t - m)
                den = jnp.sum(p, axis=-1, keepdims=True) + jnp.exp(sink - m)
                probs.append((p * (1.0 / den)).astype(BF16))
            p_all = jnp.concatenate(probs, axis=0)
            o_all = jnp.dot(p_all, vwin, preferred_element_type=F32)
            o_ref[r0:r0 + WIN, g * HEAD_GROUP_W:(g + 1) * HEAD_GROUP_W] = _diag_heads(o_all, WIN, lane_head)


def _window_attention(proj, sink, bounds):
    t_all = proj.shape[0]
    tq = ROW_TILE
    nblk = t_all // tq
    halo_per_tile = tq // WIN
    n_halo = t_all // WIN
    k_col = (QA_W + 3 * B_W) // KVA_W
    v_col = k_col + 1

    def main_spec(col):
        return pl.BlockSpec((tq, KVA_W), lambda i: (i, col))

    def prev_spec(col):
        return pl.BlockSpec((WIN, KVA_W), lambda i: (jnp.maximum(i * halo_per_tile - 1, 0), col))

    def next_spec(col):
        return pl.BlockSpec((WIN, KVA_W), lambda i: (jnp.minimum((i + 1) * halo_per_tile, n_halo - 1), col))

    return pl.pallas_call(
        functools.partial(_win_kernel, bounds=bounds),
        grid=(nblk,),
        in_specs=[
            pl.BlockSpec(memory_space=pltpu.SMEM),
            pl.BlockSpec((tq, QA_W), lambda i: (i, 0)),
            prev_spec(k_col), main_spec(k_col), next_spec(k_col),
            prev_spec(v_col), main_spec(v_col), next_spec(v_col),
        ],
        out_specs=pl.BlockSpec((tq, QA_W), lambda i: (i, 0)),
        out_shape=jax.ShapeDtypeStruct((t_all, QA_W), F32),
        scratch_shapes=[pltpu.VMEM((N_KV_A, tq + 2 * WIN, KVA_W), BF16), pltpu.VMEM((N_KV_A, tq + 2 * WIN, KVA_W), BF16),
                        pltpu.VMEM((3, N_HEADS_A, WIN, 3 * WIN), F32)],
        compiler_params=_cparams(("arbitrary",), VMEM_LIMIT),
        name="window_attention",
    )(sink, proj, proj, proj, proj, proj, proj, proj)


def _bias_table_kernel(rpb_ref, o_ref):
    j0 = pl.program_id(0)
    g = pl.program_id(1)
    n_rel_rows = 2 * NA_ROWS - 1
    n_rel_cols = 2 * NA_COLS - 1
    qc = lax.broadcasted_iota(I32, (GRID_W, 2 * GRID_W), 0)
    lane = lax.broadcasted_iota(I32, (GRID_W, 2 * GRID_W), 1)
    kc = lane % GRID_W
    second = lane >= GRID_W
    rel = jnp.clip(kc - qc + NA_COLS - 1, 0, n_rel_cols - 1)
    col_start = jnp.clip(qc - NA_COLS // 2, 0, GRID_W - NA_COLS)
    col_valid = (kc >= col_start) & (kc < col_start + NA_COLS)
    for r in range(4):
        h = g * 4 + r
        for kp in range(NA_ROWS // 2):
            base_a = (h * n_rel_rows + j0 + 2 * kp) * n_rel_cols
            base_b = base_a + n_rel_cols

            tile = jnp.zeros((GRID_W, 2 * GRID_W), F32)
            for d in range(n_rel_cols):
                val = jnp.where(second, rpb_ref[base_b + d], rpb_ref[base_a + d])
                tile = jnp.where(rel == d, val, tile)
            o_ref[r * GRID_W:(r + 1) * GRID_W, kp * 2 * GRID_W:(kp + 1) * 2 * GRID_W] = jnp.where(col_valid, tile, -jnp.inf)


def _bias_table(rpb):
    n_j0 = NA_ROWS
    return pl.pallas_call(
        _bias_table_kernel,
        grid=(n_j0, 4),
        in_specs=[pl.BlockSpec(memory_space=pltpu.SMEM)],
        out_specs=pl.BlockSpec((None, None, 4 * GRID_W, NA_ROWS * GRID_W), lambda j, g: (j, g, 0, 0)),
        out_shape=jax.ShapeDtypeStruct((n_j0, 4, 4 * GRID_W, NA_ROWS * GRID_W), F32),
        compiler_params=_cparams(("arbitrary", "arbitrary")),
        name="na_bias_table",
    )(rpb.reshape(-1))


def _na_kernel(q_ref, kp_ref, km_ref, kn_ref, vp_ref, vm_ref, vn_ref, bt_ref, o_ref, kbuf, vbuf, *, row_bounds):
    i = pl.program_id(1)
    rows = NA_ROWS_PER_STEP
    tile = rows * GRID_W
    lane_head = _lane_head(HEAD_GROUP_W)
    for buf, (p_ref, m_ref, n_ref) in ((kbuf, (kp_ref, km_ref, kn_ref)), (vbuf, (vp_ref, vm_ref, vn_ref))):
        buf[0:tile, :] = p_ref[...]
        buf[tile:2 * tile, :] = m_ref[...]
        buf[2 * tile:3 * tile, :] = n_ref[...]
    row0 = i * rows
    lo, hi = _seq_bounds(row0, row_bounds)

    for jr in range(rows):
        r = row0 + jr
        rs = jnp.clip(r - NA_ROWS // 2, lo, hi - NA_ROWS)
        j0 = rs - r + NA_ROWS - 1
        off = pl.multiple_of((rs - row0 + rows) * GRID_W, GRID_W)
        q0 = jr * GRID_W
        q = q_ref[q0:q0 + GRID_W, :] * Q_SCALE
        lhs = _stack_heads(q, lane_head)
        kwin = kbuf[pl.ds(off, NA_ROWS * GRID_W), :]
        vwin = vbuf[pl.ds(off, NA_ROWS * GRID_W), :]
        s = lax.dot_general(lhs, kwin, _NT, preferred_element_type=F32) + bt_ref[j0]
        m = jnp.max(s, axis=-1, keepdims=True)
        p = jnp.exp(s - m)
        den = jnp.sum(p, axis=-1, keepdims=True)
        pb = (p * (1.0 / den)).astype(BF16)
        o_all = jnp.dot(pb, vwin, preferred_element_type=F32)
        o_ref[q0:q0 + GRID_W, :] = _diag_heads(o_all, GRID_W, lane_head)


def _neighborhood_attention(proj, bias_table, bounds):
    t_all = proj.shape[0]
    tile = NA_ROWS_PER_STEP * GRID_W
    nblk = t_all // tile
    row_bounds = tuple(b // GRID_W for b in bounds)
    q_col = QA_W // HEAD_GROUP_W
    k_col = (QA_W + B_W) // HEAD_GROUP_W
    v_col = (QA_W + 2 * B_W) // HEAD_GROUP_W

    def spec(col, shift):
        return pl.BlockSpec((tile, HEAD_GROUP_W), lambda g, i: (jnp.clip(i + shift, 0, nblk - 1), col + g))

    return pl.pallas_call(
        functools.partial(_na_kernel, row_bounds=row_bounds),
        grid=(4, nblk),
        in_specs=[
            spec(q_col, 0),
            spec(k_col, -1), spec(k_col, 0), spec(k_col, 1),
            spec(v_col, -1), spec(v_col, 0), spec(v_col, 1),
            pl.BlockSpec((NA_ROWS, None, 4 * GRID_W, NA_ROWS * GRID_W), lambda g, i: (0, g, 0, 0)),
        ],
        out_specs=pl.BlockSpec((tile, HEAD_GROUP_W), lambda g, i: (i, g)),
        out_shape=jax.ShapeDtypeStruct((t_all, B_W), F32),
        scratch_shapes=[pltpu.VMEM((3 * tile, HEAD_GROUP_W), BF16), pltpu.VMEM((3 * tile, HEAD_GROUP_W), BF16)],
        compiler_params=_cparams(("arbitrary", "arbitrary"), VMEM_LIMIT),
        name="neighborhood_attention",
    )(proj, proj, proj, proj, proj, proj, proj, bias_table)


def _split_bf16(x):
    hi = x.astype(BF16)
    lo = (x - hi.astype(F32)).astype(BF16)
    return hi, lo


def _out_proj_body(x_ref, oa_ref, ob_ref, gmix_ref, wout_ref, gffn_ref, wr_ref, x1_ref, h2_ref, lg_ref):
    ma = (_rms_scale(oa_ref[...]) * gmix_ref[:, :QA_W]).astype(BF16)
    mb = (_rms_scale(ob_ref[...]) * gmix_ref[:, QA_W:]).astype(BF16)
    y = jnp.dot(ma, wout_ref[:QA_W, :], preferred_element_type=F32)
    y = y + jnp.dot(mb, wout_ref[QA_W:, :], preferred_element_type=F32)
    x1 = x_ref[...] + y
    x1_ref[...] = x1
    h2 = _rms_scale(x1) * gffn_ref[...]
    h2_ref[...] = h2
    h_hi, h_lo = _split_bf16(h2)
    w_hi, w_lo = _split_bf16(wr_ref[...])
    lg = lax.dot_general(w_hi, h_hi, _NT, preferred_element_type=F32)
    lg = lg + lax.dot_general(w_lo, h_hi, _NT, preferred_element_type=F32)
    lg = lg + lax.dot_general(w_hi, h_lo, _NT, preferred_element_type=F32)
    lg_ref[...] = lg


def _out_proj_kernel(xp_ref, xs_ref, *rest, n_prompt_blocks):
    i = pl.program_id(0)

    @pl.when(i < n_prompt_blocks)
    def _():
        _out_proj_body(xp_ref, *rest)

    @pl.when(i >= n_prompt_blocks)
    def _():
        _out_proj_body(xs_ref, *rest)


def _out_proj(xp, xs, o_a, o_b, g_mix, w_out_bf16, g_ffn, w_router):
    tp, ts = xp.shape[0], xs.shape[0]
    t_all = tp + ts
    npb, nsb = tp // ROW_TILE, ts // ROW_TILE
    row = lambda i: (i, 0)
    const = lambda i: (0, 0)
    return pl.pallas_call(
        functools.partial(_out_proj_kernel, n_prompt_blocks=npb),
        grid=(npb + nsb,),
        in_specs=[
            pl.BlockSpec((ROW_TILE, D_MODEL), lambda i: (jnp.minimum(i, npb - 1), 0)),
            pl.BlockSpec((ROW_TILE, D_MODEL), lambda i: (jnp.maximum(i - npb, 0), 0)),
            pl.BlockSpec((ROW_TILE, QA_W), row),
            pl.BlockSpec((ROW_TILE, B_W), row),
            pl.BlockSpec((1, D_MODEL), const),
            pl.BlockSpec((D_MODEL, D_MODEL), const, pipeline_mode=pl.Buffered(1)),
            pl.BlockSpec((1, D_MODEL), const),
            pl.BlockSpec((ROUTER_ROWS, D_MODEL), const),
        ],
        out_specs=[
            pl.BlockSpec((ROW_TILE, D_MODEL), row),
            pl.BlockSpec((ROW_TILE, D_MODEL), row),
            pl.BlockSpec((ROUTER_ROWS, ROW_TILE), lambda i: (0, i)),
        ],
        out_shape=[
            jax.ShapeDtypeStruct((t_all, D_MODEL), F32),
            jax.ShapeDtypeStruct((t_all, D_MODEL), F32),
            jax.ShapeDtypeStruct((ROUTER_ROWS, t_all), F32),
        ],
        compiler_params=_cparams(("arbitrary",), VMEM_LIMIT),
        name="out_proj_router",
    )(xp, xs, o_a, o_b, g_mix.reshape(1, D_MODEL), w_out_bf16, g_ffn.reshape(1, D_MODEL), w_router)


def _route_kernel(l_ref, dest_ref, gate_ref, blk_ref, pad_ref, pref_scr, *, t_all, n_blk, n_slots):
    chunk = 256
    lanes = 128
    sub8 = lax.broadcasted_iota(I32, (EPG, t_all), 0)
    gl = jnp.where(sub8 < N_GROUPS, l_ref[N_EXPERTS:N_EXPERTS + EPG, :], -jnp.inf)
    gmax = jnp.max(gl, axis=0, keepdims=True)
    gidx = jnp.min(jnp.where(gl == gmax, sub8, EPG), axis=0, keepdims=True)
    p_grp = 1.0 / jnp.sum(jnp.exp(gl - gmax), axis=0, keepdims=True)

    e_sel = l_ref[(N_GROUPS - 1) * EPG:N_GROUPS * EPG, :]
    for g in range(N_GROUPS - 2, -1, -1):
        e_sel = jnp.where(gidx == g, l_ref[g * EPG:(g + 1) * EPG, :], e_sel)
    ee = jnp.exp(e_sel - jnp.max(e_sel, axis=0, keepdims=True))
    pe = ee * (1.0 / jnp.sum(ee, axis=0, keepdims=True))
    v1 = jnp.max(pe, axis=0, keepdims=True)
    i1 = jnp.min(jnp.where(pe == v1, sub8, EPG), axis=0, keepdims=True)
    pe2 = jnp.where(sub8 == i1, -1.0, pe)
    v2 = jnp.max(pe2, axis=0, keepdims=True)
    i2 = jnp.min(jnp.where(pe2 == v2, sub8, EPG), axis=0, keepdims=True)
    norm = p_grp * (1.0 / (v1 + v2))
    gate_ref[...] = jnp.where(sub8 == 0, v1 * norm, jnp.where(sub8 == 1, v2 * norm, 0.0))
    e1 = gidx * EPG + i1
    e2 = gidx * EPG + i2

    tri = jnp.where(lax.broadcasted_iota(I32, (chunk, chunk), 0) < lax.broadcasted_iota(I32, (chunk, chunk), 1),
                    1.0, 0.0).astype(BF16)
    sub32 = lax.broadcasted_iota(I32, (N_EXPERTS, chunk), 0)
    count = jnp.zeros((N_EXPERTS, 1), F32)
    for c in range(t_all // chunk):
        sl = slice(c * chunk, (c + 1) * chunk)
        member = jnp.where((sub32 == e1[:, sl]) | (sub32 == e2[:, sl]), 1.0, 0.0)
        pref_scr[:, sl] = jnp.dot(member.astype(BF16), tri, preferred_element_type=F32) + count
        count = count + jnp.sum(member, axis=1, keepdims=True)

    nblk_e = jnp.floor((count + (MOE_BLK - 1)) * (1.0 / MOE_BLK))
    nb_l = jnp.broadcast_to(nblk_e, (N_EXPERTS, lanes))
    hi16 = jnp.floor(nb_l * (1.0 / 16.0))
    lo16 = nb_l - 16.0 * hi16
    ltri = jnp.where(lax.broadcasted_iota(I32, (N_EXPERTS, N_EXPERTS), 1) < lax.broadcasted_iota(I32, (N_EXPERTS, N_EXPERTS), 0),
                     1.0, 0.0).astype(BF16)
    first_blk = (16.0 * jnp.dot(ltri, hi16.astype(BF16), preferred_element_type=F32)
                 + jnp.dot(ltri, lo16.astype(BF16), preferred_element_type=F32))
    end_blk = first_blk + nb_l
    pstart = first_blk[:, 0:1] * float(MOE_BLK)

    sub32t = lax.broadcasted_iota(I32, (N_EXPERTS, t_all), 0)
    slot = pref_scr[...] + pstart
    d1 = jnp.sum(jnp.where(sub32t == e1, slot, 0.0), axis=0, keepdims=True)
    d2 = jnp.sum(jnp.where(sub32t == e2, slot, 0.0), axis=0, keepdims=True)
    dest_ref[...] = jnp.where(sub8 == 0, d1, jnp.where(sub8 == 1, d2, 0.0)).astype(I32)

    blk_i = lax.broadcasted_iota(I32, (N_EXPERTS, n_blk), 1).astype(F32)
    owner = jnp.sum(jnp.where(end_blk[:, 0:1] <= blk_i, 1.0, 0.0), axis=0, keepdims=True)
    owner = jnp.minimum(owner, float(N_EXPERTS - 1))
    n_used = jnp.sum(nblk_e, axis=0, keepdims=True)
    row8 = lax.broadcasted_iota(I32, (8, n_blk), 0)
    blk_ref[...] = jnp.where(row8 == 0, owner, jnp.where(row8 == 1, n_used, 0.0)).astype(I32)
    pad_ref[0:N_EXPERTS, :] = (first_blk * float(MOE_BLK) + jnp.broadcast_to(count, (N_EXPERTS, lanes))).astype(I32)
    last = lax.broadcasted_iota(I32, (N_EXPERTS, lanes), 0) == N_EXPERTS - 1
    pad_ref[N_EXPERTS:, :] = jnp.where(last, float(n_slots), end_blk * float(MOE_BLK)).astype(I32)


def _route(logits_t, n_slots):
    t_all = logits_t.shape[1]
    n_blk = pl.cdiv(n_slots // MOE_BLK, 128) * 128
    vmem = pl.BlockSpec(memory_space=pltpu.VMEM)
    return pl.pallas_call(
        functools.partial(_route_kernel, t_all=t_all, n_blk=n_blk, n_slots=n_slots),
        in_specs=[vmem],
        out_specs=[vmem, vmem, vmem, vmem],
        out_shape=[
            jax.ShapeDtypeStruct((8, t_all), I32),
            jax.ShapeDtypeStruct((8, t_all), F32),
            jax.ShapeDtypeStruct((8, n_blk), I32),
            jax.ShapeDtypeStruct((2 * N_EXPERTS, 128), I32),
        ],
        scratch_shapes=[pltpu.VMEM((N_EXPERTS, t_all), F32)],
        compiler_params=pltpu.CompilerParams(vmem_limit_bytes=VMEM_LIMIT),
        name="route",
    )(logits_t)


def _slot_map_kernel(d1_ref, d2_ref, padlo_ref, padhi_ref, slot_ref, *, t_all, t_pad):
    def per_expert(e, c):
        def fill(s, c2):
            slot_ref[s] = t_all + jnp.bitwise_and(s, 2 * MOE_BLK - 1)
            return c2

        lax.fori_loop(padlo_ref[e], padhi_ref[e], fill, 0)
        return c

    lax.fori_loop(0, N_EXPERTS, per_expert, 0)
    unroll = 8

    def body(j, c):
        t0 = j * unroll
        first = [d1_ref[t0 + u] for u in range(unroll)]
        second = [d2_ref[t0 + u] for u in range(unroll)]
        for u in range(unroll):
            slot_ref[first[u]] = t0 + u
            slot_ref[second[u]] = t_pad + t0 + u
        return c

    lax.fori_loop(0, t_all // unroll, body, 0)


def _slot_map(d1, d2, pad_lo, pad_hi, n_slots, t_pad):
    smem = pl.BlockSpec(memory_space=pltpu.SMEM)
    return pl.pallas_call(
        functools.partial(_slot_map_kernel, t_all=d1.shape[0], t_pad=t_pad),
        in_specs=[smem, smem, smem, smem],
        out_specs=smem,
        out_shape=jax.ShapeDtypeStruct((n_slots,), I32),
        name="slot_map",
    )(d1, d2, pad_lo, pad_hi)


def _expert_changed(b, blk_e_ref):
    prev = blk_e_ref[jnp.maximum(b - 1, 0)]
    return (b == 0) | (blk_e_ref[b] != prev)


def _gather_start(slot_ref, h_hbm, xbuf, sem, blk, buf, r, t_all, t_pad):
    row = slot_ref[blk * MOE_BLK + r]
    tok = jnp.minimum(jnp.where(row >= t_pad, row - t_pad, row), t_all - 1)
    pltpu.make_async_copy(h_hbm.at[pl.ds(tok, 1), :], xbuf.at[buf, pl.ds(r, 1), :], sem.at[buf]).start()


def _gather_wait(h_hbm, xbuf, sem, buf):
    for r in range(MOE_BLK):
        pltpu.make_async_copy(h_hbm.at[pl.ds(0, 1), :], xbuf.at[buf, pl.ds(r, 1), :], sem.at[buf]).wait()


def _moe_up_kernel(blk_e_ref, nused_ref, slot_ref, h_hbm, wg_ref, wu_ref, hid_ref, xbuf, sem, wg_bf, wu_bf, x_bf,
                   *, t_all, t_pad):
    b = pl.program_id(0)
    nused = nused_ref[0]
    cur = lax.rem(b, 2)
    n_chunks = 4
    rows_per_chunk = MOE_BLK // n_chunks
    nch = D_EXPERT // n_chunks

    @pl.when(b == 0)
    def _():
        for r in range(MOE_BLK):
            _gather_start(slot_ref, h_hbm, xbuf, sem, 0, 0, r, t_all, t_pad)

    @pl.when(b < nused)
    def _():
        _gather_wait(h_hbm, xbuf, sem, cur)

        @pl.when(_expert_changed(b, blk_e_ref))
        def _():
            wg_bf[...] = wg_ref[...].astype(BF16)
            wu_bf[...] = wu_ref[...].astype(BF16)

        x_bf[...] = xbuf[cur].astype(BF16)

    for j in range(n_chunks):
        @pl.when((b < nused) & (blk_e_ref[b] >= -j))
        def _(j=j):
            nxt = jnp.minimum(b + 1, nused - 1)
            for r in range(j * rows_per_chunk, (j + 1) * rows_per_chunk):
                _gather_start(slot_ref, h_hbm, xbuf, sem, nxt, 1 - cur, r, t_all, t_pad)
            sl = slice(j * nch, (j + 1) * nch)
            x = x_bf[...]
            gt = jnp.dot(x, wg_bf[:, sl], preferred_element_type=F32)
            up = jnp.dot(x, wu_bf[:, sl], preferred_element_type=F32)
            hid_ref[:, sl] = (gt * (1.0 / (1.0 + jnp.exp(-gt))) * up).astype(BF16)

    @pl.when(b == nused - 1)
    def _():
        _gather_wait(h_hbm, xbuf, sem, 1 - cur)

    @pl.when(b >= nused)
    def _():
        hid_ref[...] = jnp.zeros_like(hid_ref)


def _scatter_start(slot_ref, ybuf, y_hbm, sem, blk, buf, r):
    row = slot_ref[blk * MOE_BLK + r]
    pltpu.make_async_copy(ybuf.at[buf, pl.ds(r, 1), :], y_hbm.at[pl.ds(row, 1), :], sem.at[buf]).start()


def _scatter_wait(ybuf, y_hbm, sem, buf):
    for r in range(MOE_BLK):
        pltpu.make_async_copy(ybuf.at[buf, pl.ds(r, 1), :], y_hbm.at[pl.ds(0, 1), :], sem.at[buf]).wait()


def _moe_down_kernel(blk_e_ref, nused_ref, slot_ref, hid_ref, wd_ref, y_hbm, ybuf, sem, wd_bf, *, t_all, t_pad):
    b = pl.program_id(0)
    nused = nused_ref[0]
    cur = lax.rem(b, 2)
    n_chunks = 4
    rows_per_chunk = MOE_BLK // n_chunks
    nch = D_MODEL // n_chunks

    @pl.when(b == 0)
    def _():
        ybuf[0] = jnp.zeros((MOE_BLK, D_MODEL), F32)
        for half in range(2):
            for part in range(2):
                row0 = half * t_pad + t_all + part * MOE_BLK
                cp = pltpu.make_async_copy(ybuf.at[0], y_hbm.at[pl.ds(row0, MOE_BLK), :], sem.at[0])
                cp.start()
                cp.wait()

    @pl.when((b >= 2) & (b <= nused))
    def _():
        _scatter_wait(ybuf, y_hbm, sem, cur)

    def compute(scatter_previous):
        h = hid_ref[...]
        for j in range(n_chunks):
            if scatter_previous:
                for r in range(j * rows_per_chunk, (j + 1) * rows_per_chunk):
                    _scatter_start(slot_ref, ybuf, y_hbm, sem, b - 1, 1 - cur, r)
            sl = slice(j * nch, (j + 1) * nch)
            ybuf[cur, :, sl] = jnp.dot(h, wd_bf[:, sl], preferred_element_type=F32)

    @pl.when(b == 0)
    def _():
        wd_bf[...] = wd_ref[...].astype(BF16)
        compute(False)

    @pl.when((b >= 1) & (b < nused))
    def _():
        @pl.when(_expert_changed(b, blk_e_ref))
        def _():
            wd_bf[...] = wd_ref[...].astype(BF16)

        compute(True)

    @pl.when(b == nused)
    def _():
        for r in range(MOE_BLK):
            _scatter_start(slot_ref, ybuf, y_hbm, sem, b - 1, 1 - cur, r)
        _scatter_wait(ybuf, y_hbm, sem, 1 - cur)


def _blk(b, nused_ref):
    return jnp.minimum(b, nused_ref[0] - 1)


def _moe(h2, slot, blk_e, n_used, w_gate, w_up, w_down, t_pad):
    t_all = h2.shape[0]
    n_slots = slot.shape[0]
    n_blk = n_slots // MOE_BLK
    hid = pl.pallas_call(
        functools.partial(_moe_up_kernel, t_all=t_all, t_pad=t_pad),
        grid_spec=pltpu.PrefetchScalarGridSpec(
            num_scalar_prefetch=3,
            grid=(n_blk,),
            in_specs=[
                pl.BlockSpec(memory_space=pl.ANY),
                pl.BlockSpec((None, D_MODEL, D_EXPERT), lambda b, be, nu, sl: (be[_blk(b, nu)], 0, 0)),
                pl.BlockSpec((None, D_MODEL, D_EXPERT), lambda b, be, nu, sl: (be[_blk(b, nu)], 0, 0)),
            ],
            out_specs=pl.BlockSpec((MOE_BLK, D_EXPERT), lambda b, be, nu, sl: (b, 0)),
            scratch_shapes=[pltpu.VMEM((2, MOE_BLK, D_MODEL), F32), pltpu.SemaphoreType.DMA((2,)),
                            pltpu.VMEM((D_MODEL, D_EXPERT), BF16), pltpu.VMEM((D_MODEL, D_EXPERT), BF16),
                            pltpu.VMEM((MOE_BLK, D_MODEL), BF16)],
        ),
        out_shape=jax.ShapeDtypeStruct((n_slots, D_EXPERT), BF16),
        compiler_params=_cparams(("arbitrary",), MOE_VMEM_LIMIT),
        name="moe_gate_up",
    )(blk_e, n_used, slot, h2, w_gate, w_up)
    return pl.pallas_call(
        functools.partial(_moe_down_kernel, t_all=t_all, t_pad=t_pad),
        grid_spec=pltpu.PrefetchScalarGridSpec(
            num_scalar_prefetch=3,
            grid=(n_blk,),
            in_specs=[
                pl.BlockSpec((MOE_BLK, D_EXPERT), lambda b, be, nu, sl: (_blk(b, nu), 0)),
                pl.BlockSpec((None, D_EXPERT, D_MODEL), lambda b, be, nu, sl: (be[_blk(b, nu)], 0, 0)),
            ],
            out_specs=pl.BlockSpec(memory_space=pl.ANY),
            scratch_shapes=[pltpu.VMEM((2, MOE_BLK, D_MODEL), F32), pltpu.SemaphoreType.DMA((2,)),
                            pltpu.VMEM((D_EXPERT, D_MODEL), BF16)],
        ),
        out_shape=jax.ShapeDtypeStruct((2 * t_pad, D_MODEL), F32),
        compiler_params=_cparams(("arbitrary",), VMEM_LIMIT),
        name="moe_down",
    )(blk_e, n_used, slot, hid, w_down)


def _combine_body(x1_ref, ya_ref, yb_ref, gate_ref, g_ref, out_ref):
    gt = gate_ref[...]
    x2 = x1_ref[...] + (ya_ref[...] * gt[:, 0:1] + yb_ref[...] * gt[:, 1:2])
    out_ref[...] = _rms_scale(x2) * g_ref[...]


def _combine_kernel(x1_ref, ya_ref, yb_ref, gate_ref, g_ref, outp_ref, outs_ref, *, n_prompt_blocks):
    i = pl.program_id(0)

    @pl.when(i < n_prompt_blocks)
    def _():
        _combine_body(x1_ref, ya_ref, yb_ref, gate_ref, g_ref, outp_ref)

    @pl.when(i >= n_prompt_blocks)
    def _():
        _combine_body(x1_ref, ya_ref, yb_ref, gate_ref, g_ref, outs_ref)


def _combine(x1, gates_t, g_final, y2, tp, ts, t_pad):
    tm = ROW_TILE
    npb, nsb = tp // tm, ts // tm
    second = t_pad // tm
    return pl.pallas_call(
        functools.partial(_combine_kernel, n_prompt_blocks=npb),
        grid=(npb + nsb,),
        in_specs=[
            pl.BlockSpec((tm, D_MODEL), lambda i: (i, 0)),
            pl.BlockSpec((tm, D_MODEL), lambda i: (i, 0)),
            pl.BlockSpec((tm, D_MODEL), lambda i: (second + i, 0)),
            pl.BlockSpec((tm, 2), lambda i: (i, 0)),
            pl.BlockSpec((1, D_MODEL), lambda i: (0, 0)),
        ],
        out_specs=[
            pl.BlockSpec((tm, D_MODEL), lambda i: (jnp.minimum(i, npb - 1), 0)),
            pl.BlockSpec((tm, D_MODEL), lambda i: (jnp.maximum(i - npb, 0), 0)),
        ],
        out_shape=[jax.ShapeDtypeStruct((tp, D_MODEL), F32), jax.ShapeDtypeStruct((ts, D_MODEL), F32)],
        compiler_params=_cparams(("arbitrary",), VMEM_LIMIT),
        name="combine_final_norm",
    )(x1, y2, y2, gates_t, g_final.reshape(1, D_MODEL))


def _layer(xp, xs, g_attn, w_in, g_mix, sink_a, rpb_b, w_out, g_ffn, w_rg, w_re, w_gate, w_up, w_down, g_final):
    tp, ts = xp.shape[0], xs.shape[0]
    t_all = tp + ts
    bounds = (0, tp, t_all)
    s0, s1, s2, s3, s4 = QA_W, QA_W + KVA_W, QA_W + 2 * KVA_W, QA_W + 2 * KVA_W + B_W, QA_W + 2 * KVA_W + 2 * B_W
    w_in_p = jnp.concatenate([w_in[:, :s0], w_in[:, s2:s3], w_in[:, s3:s4], w_in[:, s4:], w_in[:, s0:s1], w_in[:, s1:s2]],
                             axis=1).astype(BF16)
    w_router = jnp.concatenate([w_re.T, w_rg.T, jnp.zeros((ROUTER_ROWS - N_EXPERTS - N_GROUPS, D_MODEL), F32)], axis=0)

    proj = _in_proj(xp, xs, g_attn, w_in_p)
    o_a = _window_attention(proj, sink_a, bounds)
    o_b = _neighborhood_attention(proj, _bias_table(rpb_b), bounds)
    x1, h2, logits_t = _out_proj(xp, xs, o_a, o_b, g_mix, w_out.astype(BF16), g_ffn, w_router)

    n_slots = 2 * t_all + N_EXPERTS * MOE_BLK
    n_blk = n_slots // MOE_BLK
    t_pad = t_all + 2 * MOE_BLK
    dest, gate, blk, pad = _route(logits_t, n_slots)
    slot = _slot_map(dest[0], dest[1], pad[:N_EXPERTS, 0], pad[N_EXPERTS:, 0], n_slots, t_pad)
    y2 = _moe(h2, slot, blk[0, :n_blk], blk[1, :1], w_gate, w_up, w_down, t_pad)
    return _combine(x1, gate[:2].T, g_final, y2, tp, ts, t_pad)


def kernel(x_prompt, x_sample, g_attn, w_in, g_mix, sink_a, rpb_b, w_out, g_ffn, w_router_group, w_router_expert,
           w_gate, w_up, w_down, g_final):
    assert x_prompt.shape[0] == 1 and x_sample.shape[0] == 1 and g_attn.shape[0] == 1
    yp, ys = _layer(x_prompt[0], x_sample[0], g_attn[0], w_in[0], g_mix[0], sink_a[0], rpb_b[0], w_out[0], g_ffn[0],
                    w_router_group[0], w_router_expert[0], w_gate[0], w_up[0], w_down[0], g_final)
    return yp[None], ys[None]
```

```python
import functools

import jax
import jax.numpy as jnp
from jax import lax
from jax.experimental import pallas as pl
from jax.experimental.pallas import tpu as pltpu

F32 = jnp.float32
BF16 = jnp.bfloat16
I32 = jnp.int32

D_MODEL = 2048
HEAD_DIM = 64
N_HEADS_A = 16
N_KV_A = 4
N_HEADS_B = 16
WIN = 128
GRID_W = 64
NA_ROWS = 8
NA_COLS = 16
N_GROUPS = 4
EPG = 8
N_EXPERTS = 32
D_EXPERT = 1024
EPS = 1e-6

QA_W = N_HEADS_A * HEAD_DIM
KVA_W = N_KV_A * HEAD_DIM
B_W = N_HEADS_B * HEAD_DIM
D_IN = QA_W + 2 * KVA_W + 3 * B_W
HEAD_GROUP_W = 256
Q_SCALE = HEAD_DIM ** -0.5

ROW_TILE = 512
NA_ROWS_PER_STEP = 8
MOE_BLK = 512
ROUTER_ROWS = 40
LANES = 128
ROW_SLABS = D_MODEL // LANES
VMEM_LIMIT = 56 * 1024 * 1024
MOE_VMEM_LIMIT = 60 * 1024 * 1024


def _cparams(sem, vmem=None):
    return pltpu.CompilerParams(dimension_semantics=sem, vmem_limit_bytes=vmem)


def _rms_scale(x):
    return x * lax.rsqrt(jnp.mean(x * x, axis=-1, keepdims=True) + EPS)


def _store_row_slabs(ref, x):
    rows = x.shape[0]
    for j in range(ROW_SLABS):
        ref[pl.ds(j, rows, stride=ROW_SLABS), :] = x[:, j * LANES:(j + 1) * LANES]


def _load_row_slabs(ref, rows):
    return jnp.concatenate([ref[pl.ds(j, rows, stride=ROW_SLABS), :] for j in range(ROW_SLABS)], axis=1)


def _in_proj_body(x_ref, g_ref, w_ref, o_ref):
    h = (_rms_scale(x_ref[...]) * g_ref[...]).astype(BF16)
    nch = 512
    for j in range(D_IN // nch):
        sl = slice(j * nch, (j + 1) * nch)
        o_ref[:, sl] = jnp.dot(h, w_ref[:, sl], preferred_element_type=F32).astype(BF16)


def _in_proj_kernel(xp_ref, xs_ref, g_ref, w_ref, o_ref, *, n_prompt_blocks):
    i = pl.program_id(0)

    @pl.when(i < n_prompt_blocks)
    def _():
        _in_proj_body(xp_ref, g_ref, w_ref, o_ref)

    @pl.when(i >= n_prompt_blocks)
    def _():
        _in_proj_body(xs_ref, g_ref, w_ref, o_ref)


def _in_proj(xp, xs, g_attn, w_in_bf16):
    tp, ts = xp.shape[0], xs.shape[0]
    npb, nsb = tp // ROW_TILE, ts // ROW_TILE
    return pl.pallas_call(
        functools.partial(_in_proj_kernel, n_prompt_blocks=npb),
        grid=(npb + nsb,),
        in_specs=[
            pl.BlockSpec((ROW_TILE, D_MODEL), lambda i: (jnp.minimum(i, npb - 1), 0)),
            pl.BlockSpec((ROW_TILE, D_MODEL), lambda i: (jnp.maximum(i - npb, 0), 0)),
            pl.BlockSpec((1, D_MODEL), lambda i: (0, 0)),
            pl.BlockSpec((D_MODEL, D_IN), lambda i: (0, 0), pipeline_mode=pl.Buffered(1)),
        ],
        out_specs=pl.BlockSpec((ROW_TILE, D_IN), lambda i: (i, 0)),
        out_shape=jax.ShapeDtypeStruct((tp + ts, D_IN), BF16),
        compiler_params=_cparams(("arbitrary",), VMEM_LIMIT),
        name="in_proj",
    )(xp, xs, g_attn.reshape(1, D_MODEL), w_in_bf16)


def _seq_bounds(pos, bounds):
    lo = jnp.int32(bounds[0])
    hi = jnp.int32(bounds[1])
    for k in range(1, len(bounds) - 1):
        inside = pos >= bounds[k]
        lo = jnp.where(inside, bounds[k], lo)
        hi = jnp.where(inside, bounds[k + 1], hi)
    return lo, hi


def _lane_head(width):
    return lax.broadcasted_iota(I32, (1, width), 1) // HEAD_DIM


def _stack_heads(x, lane_head):
    zero = jnp.zeros_like(x)
    return jnp.concatenate([jnp.where(lane_head == r, x, zero) for r in range(4)], axis=0)


def _diag_heads(o_all, m, lane_head):
    out = jnp.where(lane_head == 0, o_all[0:m], 0.0)
    for r in range(1, 4):
        out = out + jnp.where(lane_head == r, o_all[r * m:(r + 1) * m], 0.0)
    return out


_NT = (((1,), (1,)), ((), ()))


def _win_kernel(sink_ref, q_ref, kp_ref, km_ref, kn_ref, vp_ref, vm_ref, vn_ref, o_ref,
                kbuf, vbuf, bias_scr, *, bounds):
    i = pl.program_id(0)
    tq = q_ref.shape[0]
    n_sub = tq // WIN
    tok0 = i * tq
    lo, hi = _seq_bounds(tok0, bounds)
    lane_head = _lane_head(HEAD_GROUP_W)

    @pl.when(i == 0)
    def _():
        qi = lax.broadcasted_iota(I32, (WIN, 3 * WIN), 0)
        ki = lax.broadcasted_iota(I32, (WIN, 3 * WIN), 1)
        dist_i = jnp.abs(qi - ki + WIN)
        dist = dist_i.astype(F32)
        for h in range(N_HEADS_A):
            slope = 2.0 ** (-8.0 * (h + 1.0) / N_HEADS_A)
            base = jnp.where(dist_i <= WIN, -(slope * dist), -jnp.inf)
            bias_scr[0, h] = base
            bias_scr[1, h] = jnp.where(ki >= WIN, base, -jnp.inf)
            bias_scr[2, h] = jnp.where(ki < 2 * WIN, base, -jnp.inf)

    er = lax.broadcasted_iota(I32, (HEAD_GROUP_W, HEAD_GROUP_W), 0)
    ec = lax.broadcasted_iota(I32, (HEAD_GROUP_W, HEAD_GROUP_W), 1)
    for g in range(N_KV_A):
        spread = jnp.where((er // HEAD_DIM == g) & (er % HEAD_DIM == ec % HEAD_DIM), 1.0, 0.0).astype(BF16)
        for buf, (p_ref, m_ref, n_ref) in ((kbuf, (kp_ref, km_ref, kn_ref)), (vbuf, (vp_ref, vm_ref, vn_ref))):
            buf[g, 0:WIN, :] = jnp.dot(p_ref[...], spread, preferred_element_type=F32).astype(BF16)
            buf[g, WIN:WIN + tq, :] = jnp.dot(m_ref[...], spread, preferred_element_type=F32).astype(BF16)
            buf[g, WIN + tq:, :] = jnp.dot(n_ref[...], spread, preferred_element_type=F32).astype(BF16)

    for g in range(N_KV_A):
        for c in range(n_sub):
            r0 = c * WIN
            qg = q_ref[r0:r0 + WIN, g * HEAD_GROUP_W:(g + 1) * HEAD_GROUP_W] * Q_SCALE
            lhs = _stack_heads(qg, lane_head)
            kwin = kbuf[g, r0:r0 + 3 * WIN, :]
            vwin = vbuf[g, r0:r0 + 3 * WIN, :]
            s = lax.dot_general(lhs, kwin, _NT, preferred_element_type=F32)
            variant = 0
            if c == 0:
                variant = jnp.where(tok0 == lo, 1, 0)
            if c == n_sub - 1:
                variant = jnp.where(tok0 + tq == hi, 2, 0)
            probs = []
            for r in range(4):
                h = g * 4 + r
                sink = sink_ref[h]
                t = s[r * WIN:(r + 1) * WIN] + bias_scr[variant, h]
                m = jnp.maximum(jnp.max(t, axis=-1, keepdims=True), sink)
                p = jnp.exp(t - m)
                den = jnp.sum(p, axis=-1, keepdims=True) + jnp.exp(sink - m)
                probs.append((p * (1.0 / den)).astype(BF16))
            p_all = jnp.concatenate(probs, axis=0)
            o_all = jnp.dot(p_all, vwin, preferred_element_type=F32)
            o_ref[r0:r0 + WIN, g * HEAD_GROUP_W:(g + 1) * HEAD_GROUP_W] = _diag_heads(o_all, WIN, lane_head)


def _window_attention(proj, sink, bounds):
    t_all = proj.shape[0]
    tq = ROW_TILE
    nblk = t_all // tq
    halo_per_tile = tq // WIN
    n_halo = t_all // WIN
    k_col = (QA_W + 3 * B_W) // KVA_W
    v_col = k_col + 1

    def main_spec(col):
        return pl.BlockSpec((tq, KVA_W), lambda i: (i, col))

    def prev_spec(col):
        return pl.BlockSpec((WIN, KVA_W), lambda i: (jnp.maximum(i * halo_per_tile - 1, 0), col))

    def next_spec(col):
        return pl.BlockSpec((WIN, KVA_W), lambda i: (jnp.minimum((i + 1) * halo_per_tile, n_halo - 1), col))

    return pl.pallas_call(
        functools.partial(_win_kernel, bounds=bounds),
        grid=(nblk,),
        in_specs=[
            pl.BlockSpec(memory_space=pltpu.SMEM),
            pl.BlockSpec((tq, QA_W), lambda i: (i, 0)),
            prev_spec(k_col), main_spec(k_col), next_spec(k_col),
            prev_spec(v_col), main_spec(v_col), next_spec(v_col),
        ],
        out_specs=pl.BlockSpec((tq, QA_W), lambda i: (i, 0)),
        out_shape=jax.ShapeDtypeStruct((t_all, QA_W), F32),
        scratch_shapes=[pltpu.VMEM((N_KV_A, tq + 2 * WIN, KVA_W), BF16), pltpu.VMEM((N_KV_A, tq + 2 * WIN, KVA_W), BF16),
                        pltpu.VMEM((3, N_HEADS_A, WIN, 3 * WIN), F32)],
        compiler_params=_cparams(("arbitrary",), VMEM_LIMIT),
        name="window_attention",
    )(sink, proj, proj, proj, proj, proj, proj, proj)


def _bias_table_kernel(rpb_ref, o_ref):
    j0 = pl.program_id(0)
    g = pl.program_id(1)
    n_rel_rows = 2 * NA_ROWS - 1
    n_rel_cols = 2 * NA_COLS - 1
    qc = lax.broadcasted_iota(I32, (GRID_W, 2 * GRID_W), 0)
    lane = lax.broadcasted_iota(I32, (GRID_W, 2 * GRID_W), 1)
    kc = lane % GRID_W
    second = lane >= GRID_W
    rel = jnp.clip(kc - qc + NA_COLS - 1, 0, n_rel_cols - 1)
    col_start = jnp.clip(qc - NA_COLS // 2, 0, GRID_W - NA_COLS)
    col_valid = (kc >= col_start) & (kc < col_start + NA_COLS)
    for r in range(4):
        h = g * 4 + r
        for kp in range(NA_ROWS // 2):
            base_a = (h * n_rel_rows + j0 + 2 * kp) * n_rel_cols
            base_b = base_a + n_rel_cols

            tile = jnp.zeros((GRID_W, 2 * GRID_W), F32)
            for d in range(n_rel_cols):
                val = jnp.where(second, rpb_ref[base_b + d], rpb_ref[base_a + d])
                tile = jnp.where(rel == d, val, tile)
            o_ref[r * GRID_W:(r + 1) * GRID_W, kp * 2 * GRID_W:(kp + 1) * 2 * GRID_W] = jnp.where(col_valid, tile, -jnp.inf)


def _bias_table(rpb):
    n_j0 = NA_ROWS
    return pl.pallas_call(
        _bias_table_kernel,
        grid=(n_j0, 4),
        in_specs=[pl.BlockSpec(memory_space=pltpu.SMEM)],
        out_specs=pl.BlockSpec((None, None, 4 * GRID_W, NA_ROWS * GRID_W), lambda j, g: (j, g, 0, 0)),
        out_shape=jax.ShapeDtypeStruct((n_j0, 4, 4 * GRID_W, NA_ROWS * GRID_W), F32),
        compiler_params=_cparams(("arbitrary", "arbitrary")),
        name="na_bias_table",
    )(rpb.reshape(-1))


def _na_kernel(q_ref, kp_ref, km_ref, kn_ref, vp_ref, vm_ref, vn_ref, bt_ref, o_ref, kbuf, vbuf, *, row_bounds):
    i = pl.program_id(1)
    rows = NA_ROWS_PER_STEP
    tile = rows * GRID_W
    lane_head = _lane_head(HEAD_GROUP_W)
    for buf, (p_ref, m_ref, n_ref) in ((kbuf, (kp_ref, km_ref, kn_ref)), (vbuf, (vp_ref, vm_ref, vn_ref))):
        buf[0:tile, :] = p_ref[...]
        buf[tile:2 * tile, :] = m_ref[...]
        buf[2 * tile:3 * tile, :] = n_ref[...]
    row0 = i * rows
    lo, hi = _seq_bounds(row0, row_bounds)

    for jr in range(rows):
        r = row0 + jr
        rs = jnp.clip(r - NA_ROWS // 2, lo, hi - NA_ROWS)
        j0 = rs - r + NA_ROWS - 1
        off = pl.multiple_of((rs - row0 + rows) * GRID_W, GRID_W)
        q0 = jr * GRID_W
        q = q_ref[q0:q0 + GRID_W, :] * Q_SCALE
        lhs = _stack_heads(q, lane_head)
        kwin = kbuf[pl.ds(off, NA_ROWS * GRID_W), :]
        vwin = vbuf[pl.ds(off, NA_ROWS * GRID_W), :]
        s = lax.dot_general(lhs, kwin, _NT, preferred_element_type=F32) + bt_ref[j0]
        m = jnp.max(s, axis=-1, keepdims=True)
        p = jnp.exp(s - m)
        den = jnp.sum(p, axis=-1, keepdims=True)
        pb = (p * (1.0 / den)).astype(BF16)
        o_all = jnp.dot(pb, vwin, preferred_element_type=F32)
        o_ref[q0:q0 + GRID_W, :] = _diag_heads(o_all, GRID_W, lane_head)


def _neighborhood_attention(proj, bias_table, bounds):
    t_all = proj.shape[0]
    tile = NA_ROWS_PER_STEP * GRID_W
    nblk = t_all // tile
    row_bounds = tuple(b // GRID_W for b in bounds)
    q_col = QA_W // HEAD_GROUP_W
    k_col = (QA_W + B_W) // HEAD_GROUP_W
    v_col = (QA_W + 2 * B_W) // HEAD_GROUP_W

    def spec(col, shift):
        return pl.BlockSpec((tile, HEAD_GROUP_W), lambda g, i: (jnp.clip(i + shift, 0, nblk - 1), col + g))

    return pl.pallas_call(
        functools.partial(_na_kernel, row_bounds=row_bounds),
        grid=(4, nblk),
        in_specs=[
            spec(q_col, 0),
            spec(k_col, -1), spec(k_col, 0), spec(k_col, 1),
            spec(v_col, -1), spec(v_col, 0), spec(v_col, 1),
            pl.BlockSpec((NA_ROWS, None, 4 * GRID_W, NA_ROWS * GRID_W), lambda g, i: (0, g, 0, 0)),
        ],
        out_specs=pl.BlockSpec((tile, HEAD_GROUP_W), lambda g, i: (i, g)),
        out_shape=jax.ShapeDtypeStruct((t_all, B_W), F32),
        scratch_shapes=[pltpu.VMEM((3 * tile, HEAD_GROUP_W), BF16), pltpu.VMEM((3 * tile, HEAD_GROUP_W), BF16)],
        compiler_params=_cparams(("arbitrary", "arbitrary"), VMEM_LIMIT),
        name="neighborhood_attention",
    )(proj, proj, proj, proj, proj, proj, proj, bias_table)


def _split_bf16(x):
    hi = x.astype(BF16)
    lo = (x - hi.astype(F32)).astype(BF16)
    return hi, lo


def _out_proj_body(x_ref, oa_ref, ob_ref, gmix_ref, wout_ref, gffn_ref, wr_ref, x1_ref, h2_ref, lg_ref):
    ma = (_rms_scale(oa_ref[...]) * gmix_ref[:, :QA_W]).astype(BF16)
    mb = (_rms_scale(ob_ref[...]) * gmix_ref[:, QA_W:]).astype(BF16)
    y = jnp.dot(ma, wout_ref[:QA_W, :], preferred_element_type=F32)
    y = y + jnp.dot(mb, wout_ref[QA_W:, :], preferred_element_type=F32)
    x1 = x_ref[...] + y
    x1_ref[...] = x1
    h2 = _rms_scale(x1) * gffn_ref[...]
    _store_row_slabs(h2_ref, h2)
    h_hi, h_lo = _split_bf16(h2)
    w_hi, w_lo = _split_bf16(wr_ref[...])
    lg = lax.dot_general(w_hi, h_hi, _NT, preferred_element_type=F32)
    lg = lg + lax.dot_general(w_lo, h_hi, _NT, preferred_element_type=F32)
    lg = lg + lax.dot_general(w_hi, h_lo, _NT, preferred_element_type=F32)
    lg_ref[...] = lg


def _out_proj_kernel(xp_ref, xs_ref, *rest, n_prompt_blocks):
    i = pl.program_id(0)

    @pl.when(i < n_prompt_blocks)
    def _():
        _out_proj_body(xp_ref, *rest)

    @pl.when(i >= n_prompt_blocks)
    def _():
        _out_proj_body(xs_ref, *rest)


def _out_proj(xp, xs, o_a, o_b, g_mix, w_out_bf16, g_ffn, w_router):
    tp, ts = xp.shape[0], xs.shape[0]
    t_all = tp + ts
    npb, nsb = tp // ROW_TILE, ts // ROW_TILE
    row = lambda i: (i, 0)
    const = lambda i: (0, 0)
    return pl.pallas_call(
        functools.partial(_out_proj_kernel, n_prompt_blocks=npb),
        grid=(npb + nsb,),
        in_specs=[
            pl.BlockSpec((ROW_TILE, D_MODEL), lambda i: (jnp.minimum(i, npb - 1), 0)),
            pl.BlockSpec((ROW_TILE, D_MODEL), lambda i: (jnp.maximum(i - npb, 0), 0)),
            pl.BlockSpec((ROW_TILE, QA_W), row),
            pl.BlockSpec((ROW_TILE, B_W), row),
            pl.BlockSpec((1, D_MODEL), const),
            pl.BlockSpec((D_MODEL, D_MODEL), const, pipeline_mode=pl.Buffered(1)),
            pl.BlockSpec((1, D_MODEL), const),
            pl.BlockSpec((ROUTER_ROWS, D_MODEL), const),
        ],
        out_specs=[
            pl.BlockSpec((ROW_TILE, D_MODEL), row),
            pl.BlockSpec((ROW_TILE * ROW_SLABS, LANES), row),
            pl.BlockSpec((ROUTER_ROWS, ROW_TILE), lambda i: (0, i)),
        ],
        out_shape=[
            jax.ShapeDtypeStruct((t_all, D_MODEL), F32),
            jax.ShapeDtypeStruct((t_all * ROW_SLABS, LANES), F32),
            jax.ShapeDtypeStruct((ROUTER_ROWS, t_all), F32),
        ],
        compiler_params=_cparams(("arbitrary",), VMEM_LIMIT),
        name="out_proj_router",
    )(xp, xs, o_a, o_b, g_mix.reshape(1, D_MODEL), w_out_bf16, g_ffn.reshape(1, D_MODEL), w_router)


def _route_kernel(l_ref, dest_ref, gate_ref, blk_ref, pad_ref, pref_scr, *, t_all, n_blk, n_slots):
    chunk = 256
    lanes = 128
    sub8 = lax.broadcasted_iota(I32, (EPG, t_all), 0)
    gl = jnp.where(sub8 < N_GROUPS, l_ref[N_EXPERTS:N_EXPERTS + EPG, :], -jnp.inf)
    gmax = jnp.max(gl, axis=0, keepdims=True)
    gidx = jnp.min(jnp.where(gl == gmax, sub8, EPG), axis=0, keepdims=True)
    p_grp = 1.0 / jnp.sum(jnp.exp(gl - gmax), axis=0, keepdims=True)

    e_sel = l_ref[(N_GROUPS - 1) * EPG:N_GROUPS * EPG, :]
    for g in range(N_GROUPS - 2, -1, -1):
        e_sel = jnp.where(gidx == g, l_ref[g * EPG:(g + 1) * EPG, :], e_sel)
    ee = jnp.exp(e_sel - jnp.max(e_sel, axis=0, keepdims=True))
    pe = ee * (1.0 / jnp.sum(ee, axis=0, keepdims=True))
    v1 = jnp.max(pe, axis=0, keepdims=True)
    i1 = jnp.min(jnp.where(pe == v1, sub8, EPG), axis=0, keepdims=True)
    pe2 = jnp.where(sub8 == i1, -1.0, pe)
    v2 = jnp.max(pe2, axis=0, keepdims=True)
    i2 = jnp.min(jnp.where(pe2 == v2, sub8, EPG), axis=0, keepdims=True)
    norm = p_grp * (1.0 / (v1 + v2))
    gate_ref[...] = jnp.where(sub8 == 0, v1 * norm, jnp.where(sub8 == 1, v2 * norm, 0.0))
    e1 = gidx * EPG + i1
    e2 = gidx * EPG + i2

    tri = jnp.where(lax.broadcasted_iota(I32, (chunk, chunk), 0) < lax.broadcasted_iota(I32, (chunk, chunk), 1),
                    1.0, 0.0).astype(BF16)
    sub32 = lax.broadcasted_iota(I32, (N_EXPERTS, chunk), 0)
    count = jnp.zeros((N_EXPERTS, 1), F32)
    for c in range(t_all // chunk):
        sl = slice(c * chunk, (c + 1) * chunk)
        member = jnp.where((sub32 == e1[:, sl]) | (sub32 == e2[:, sl]), 1.0, 0.0)
        pref_scr[:, sl] = jnp.dot(member.astype(BF16), tri, preferred_element_type=F32) + count
        count = count + jnp.sum(member, axis=1, keepdims=True)

    nblk_e = jnp.floor((count + (MOE_BLK - 1)) * (1.0 / MOE_BLK))
    nb_l = jnp.broadcast_to(nblk_e, (N_EXPERTS, lanes))
    hi16 = jnp.floor(nb_l * (1.0 / 16.0))
    lo16 = nb_l - 16.0 * hi16
    ltri = jnp.where(lax.broadcasted_iota(I32, (N_EXPERTS, N_EXPERTS), 1) < lax.broadcasted_iota(I32, (N_EXPERTS, N_EXPERTS), 0),
                     1.0, 0.0).astype(BF16)
    first_blk = (16.0 * jnp.dot(ltri, hi16.astype(BF16), preferred_element_type=F32)
                 + jnp.dot(ltri, lo16.astype(BF16), preferred_element_type=F32))
    end_blk = first_blk + nb_l
    pstart = first_blk[:, 0:1] * float(MOE_BLK)

    sub32t = lax.broadcasted_iota(I32, (N_EXPERTS, t_all), 0)
    slot = pref_scr[...] + pstart
    d1 = jnp.sum(jnp.where(sub32t == e1, slot, 0.0), axis=0, keepdims=True)
    d2 = jnp.sum(jnp.where(sub32t == e2, slot, 0.0), axis=0, keepdims=True)
    dest_ref[...] = jnp.where(sub8 == 0, d1, jnp.where(sub8 == 1, d2, 0.0)).astype(I32)

    blk_i = lax.broadcasted_iota(I32, (N_EXPERTS, n_blk), 1).astype(F32)
    owner = jnp.sum(jnp.where(end_blk[:, 0:1] <= blk_i, 1.0, 0.0), axis=0, keepdims=True)
    owner = jnp.minimum(owner, float(N_EXPERTS - 1))
    n_used = jnp.sum(nblk_e, axis=0, keepdims=True)
    row8 = lax.broadcasted_iota(I32, (8, n_blk), 0)
    blk_ref[...] = jnp.where(row8 == 0, owner, jnp.where(row8 == 1, n_used, 0.0)).astype(I32)
    pad_ref[0:N_EXPERTS, :] = (first_blk * float(MOE_BLK) + jnp.broadcast_to(count, (N_EXPERTS, lanes))).astype(I32)
    last = lax.broadcasted_iota(I32, (N_EXPERTS, lanes), 0) == N_EXPERTS - 1
    pad_ref[N_EXPERTS:, :] = jnp.where(last, float(n_slots), end_blk * float(MOE_BLK)).astype(I32)


def _route(logits_t, n_slots):
    t_all = logits_t.shape[1]
    n_blk = pl.cdiv(n_slots // MOE_BLK, 128) * 128
    vmem = pl.BlockSpec(memory_space=pltpu.VMEM)
    return pl.pallas_call(
        functools.partial(_route_kernel, t_all=t_all, n_blk=n_blk, n_slots=n_slots),
        in_specs=[vmem],
        out_specs=[vmem, vmem, vmem, vmem],
        out_shape=[
            jax.ShapeDtypeStruct((8, t_all), I32),
            jax.ShapeDtypeStruct((8, t_all), F32),
            jax.ShapeDtypeStruct((8, n_blk), I32),
            jax.ShapeDtypeStruct((2 * N_EXPERTS, 128), I32),
        ],
        scratch_shapes=[pltpu.VMEM((N_EXPERTS, t_all), F32)],
        compiler_params=pltpu.CompilerParams(vmem_limit_bytes=VMEM_LIMIT),
        name="route",
    )(logits_t)


def _slot_map_kernel(d1_ref, d2_ref, padlo_ref, padhi_ref, slot_ref, *, t_all, t_pad):
    def per_expert(e, c):
        def fill(s, c2):
            slot_ref[s] = t_all + jnp.bitwise_and(s, 2 * MOE_BLK - 1)
            return c2

        lax.fori_loop(padlo_ref[e], padhi_ref[e], fill, 0)
        return c

    lax.fori_loop(0, N_EXPERTS, per_expert, 0)
    unroll = 8

    def body(j, c):
        t0 = j * unroll
        first = [d1_ref[t0 + u] for u in range(unroll)]
        second = [d2_ref[t0 + u] for u in range(unroll)]
        for u in range(unroll):
            slot_ref[first[u]] = t0 + u
            slot_ref[second[u]] = t_pad + t0 + u
        return c

    lax.fori_loop(0, t_all // unroll, body, 0)


def _slot_map(d1, d2, pad_lo, pad_hi, n_slots, t_pad):
    smem = pl.BlockSpec(memory_space=pltpu.SMEM)
    return pl.pallas_call(
        functools.partial(_slot_map_kernel, t_all=d1.shape[0], t_pad=t_pad),
        in_specs=[smem, smem, smem, smem],
        out_specs=smem,
        out_shape=jax.ShapeDtypeStruct((n_slots,), I32),
        name="slot_map",
    )(d1, d2, pad_lo, pad_hi)


def _expert_changed(b, blk_e_ref):
    prev = blk_e_ref[jnp.maximum(b - 1, 0)]
    return (b == 0) | (blk_e_ref[b] != prev)


def _gather_start(slot_ref, h_hbm, xbuf, sem, blk, buf, r, t_all, t_pad):
    row = slot_ref[blk * MOE_BLK + r]
    tok = jnp.minimum(jnp.where(row >= t_pad, row - t_pad, row), t_all - 1)
    src = h_hbm.at[pl.ds(pl.multiple_of(tok * ROW_SLABS, ROW_SLABS), ROW_SLABS), :]
    pltpu.make_async_copy(src, xbuf.at[buf, pl.ds(r * ROW_SLABS, ROW_SLABS), :], sem.at[buf]).start()


def _gather_wait(h_hbm, xbuf, sem, buf):
    for r in range(MOE_BLK):
        pltpu.make_async_copy(h_hbm.at[pl.ds(0, ROW_SLABS), :], xbuf.at[buf, pl.ds(r * ROW_SLABS, ROW_SLABS), :],
                              sem.at[buf]).wait()


def _moe_up_kernel(blk_e_ref, nused_ref, slot_ref, h_hbm, wg_ref, wu_ref, hid_ref, xbuf, sem, wg_bf, wu_bf, x_bf,
                   *, t_all, t_pad):
    b = pl.program_id(0)
    nused = nused_ref[0]
    cur = lax.rem(b, 2)
    n_chunks = 4
    rows_per_chunk = MOE_BLK // n_chunks
    nch = D_EXPERT // n_chunks

    @pl.when(b == 0)
    def _():
        for r in range(MOE_BLK):
            _gather_start(slot_ref, h_hbm, xbuf, sem, 0, 0, r, t_all, t_pad)

    @pl.when(b < nused)
    def _():
        _gather_wait(h_hbm, xbuf, sem, cur)

        @pl.when(_expert_changed(b, blk_e_ref))
        def _():
            wg_bf[...] = wg_ref[...].astype(BF16)
            wu_bf[...] = wu_ref[...].astype(BF16)

        nxt = jnp.minimum(b + 1, nused - 1)
        for j in range(ROW_SLABS):
            x_bf[:, j * LANES:(j + 1) * LANES] = xbuf[cur, pl.ds(j, MOE_BLK, stride=ROW_SLABS), :].astype(BF16)
        x = x_bf[...]
        for j in range(n_chunks):
            for r in range(j * rows_per_chunk, (j + 1) * rows_per_chunk):
                _gather_start(slot_ref, h_hbm, xbuf, sem, nxt, 1 - cur, r, t_all, t_pad)
            sl = slice(j * nch, (j + 1) * nch)
            gt = jnp.dot(x, wg_bf[:, sl], preferred_element_type=F32)
            up = jnp.dot(x, wu_bf[:, sl], preferred_element_type=F32)
            hid_ref[:, sl] = (gt * (1.0 / (1.0 + jnp.exp(-gt))) * up).astype(BF16)

        @pl.when(b == nused - 1)
        def _():
            _gather_wait(h_hbm, xbuf, sem, 1 - cur)

    @pl.when(b >= nused)
    def _():
        hid_ref[...] = jnp.zeros_like(hid_ref)


def _scatter_start(slot_ref, ybuf, y_hbm, sem, blk, buf, r):
    row = slot_ref[blk * MOE_BLK + r]
    dst = y_hbm.at[pl.ds(pl.multiple_of(row * ROW_SLABS, ROW_SLABS), ROW_SLABS), :]
    pltpu.make_async_copy(ybuf.at[buf, pl.ds(r * ROW_SLABS, ROW_SLABS), :], dst, sem.at[buf]).start()


def _scatter_wait(ybuf, y_hbm, sem, buf):
    for r in range(MOE_BLK):
        pltpu.make_async_copy(ybuf.at[buf, pl.ds(r * ROW_SLABS, ROW_SLABS), :], y_hbm.at[pl.ds(0, ROW_SLABS), :],
                              sem.at[buf]).wait()


def _moe_down_kernel(blk_e_ref, nused_ref, slot_ref, hid_ref, wd_ref, y_hbm, ybuf, sem, wd_bf, *, t_all, t_pad):
    b = pl.program_id(0)
    nused = nused_ref[0]
    cur = lax.rem(b, 2)
    n_chunks = 4
    rows_per_chunk = MOE_BLK // n_chunks
    nch = D_MODEL // n_chunks

    @pl.when(b == 0)
    def _():
        ybuf[0] = jnp.zeros((MOE_BLK * ROW_SLABS, LANES), F32)
        for half in range(2):
            for part in range(2):
                row0 = (half * t_pad + t_all + part * MOE_BLK) * ROW_SLABS
                cp = pltpu.make_async_copy(ybuf.at[0], y_hbm.at[pl.ds(row0, MOE_BLK * ROW_SLABS), :], sem.at[0])
                cp.start()
                cp.wait()

    @pl.when((b >= 2) & (b <= nused))
    def _():
        _scatter_wait(ybuf, y_hbm, sem, cur)

    def compute(scatter_previous):
        h = hid_ref[...]
        for j in range(n_chunks):
            if scatter_previous:
                for r in range(j * rows_per_chunk, (j + 1) * rows_per_chunk):
                    _scatter_start(slot_ref, ybuf, y_hbm, sem, b - 1, 1 - cur, r)
            y = jnp.dot(h, wd_bf[:, j * nch:(j + 1) * nch], preferred_element_type=F32)
            for jj in range(nch // LANES):
                slab = j * (nch // LANES) + jj
                ybuf[cur, pl.ds(slab, MOE_BLK, stride=ROW_SLABS), :] = y[:, jj * LANES:(jj + 1) * LANES]

    @pl.when(b == 0)
    def _():
        wd_bf[...] = wd_ref[...].astype(BF16)
        compute(False)

    @pl.when((b >= 1) & (b < nused))
    def _():
        @pl.when(_expert_changed(b, blk_e_ref))
        def _():
            wd_bf[...] = wd_ref[...].astype(BF16)

        compute(True)

    @pl.when(b == nused)
    def _():
        for r in range(MOE_BLK):
            _scatter_start(slot_ref, ybuf, y_hbm, sem, b - 1, 1 - cur, r)
        _scatter_wait(ybuf, y_hbm, sem, 1 - cur)


def _blk(b, nused_ref):
    return jnp.minimum(b, nused_ref[0] - 1)


def _moe(h2, slot, blk_e, n_used, w_gate, w_up, w_down, t_pad):
    t_all = h2.shape[0] // ROW_SLABS
    n_slots = slot.shape[0]
    n_blk = n_slots // MOE_BLK
    hid = pl.pallas_call(
        functools.partial(_moe_up_kernel, t_all=t_all, t_pad=t_pad),
        grid_spec=pltpu.PrefetchScalarGridSpec(
            num_scalar_prefetch=3,
            grid=(n_blk,),
            in_specs=[
                pl.BlockSpec(memory_space=pl.ANY),
                pl.BlockSpec((None, D_MODEL, D_EXPERT), lambda b, be, nu, sl: (be[_blk(b, nu)], 0, 0)),
                pl.BlockSpec((None, D_MODEL, D_EXPERT), lambda b, be, nu, sl: (be[_blk(b, nu)], 0, 0)),
            ],
            out_specs=pl.BlockSpec((MOE_BLK, D_EXPERT), lambda b, be, nu, sl: (b, 0)),
            scratch_shapes=[pltpu.VMEM((2, MOE_BLK * ROW_SLABS, LANES), F32), pltpu.SemaphoreType.DMA((2,)),
                            pltpu.VMEM((D_MODEL, D_EXPERT), BF16), pltpu.VMEM((D_MODEL, D_EXPERT), BF16),
                            pltpu.VMEM((MOE_BLK, D_MODEL), BF16)],
        ),
        out_shape=jax.ShapeDtypeStruct((n_slots, D_EXPERT), BF16),
        compiler_params=_cparams(("arbitrary",), MOE_VMEM_LIMIT),
        name="moe_gate_up",
    )(blk_e, n_used, slot, h2, w_gate, w_up)
    return pl.pallas_call(
        functools.partial(_moe_down_kernel, t_all=t_all, t_pad=t_pad),
        grid_spec=pltpu.PrefetchScalarGridSpec(
            num_scalar_prefetch=3,
            grid=(n_blk,),
            in_specs=[
                pl.BlockSpec((MOE_BLK, D_EXPERT), lambda b, be, nu, sl: (_blk(b, nu), 0)),
                pl.BlockSpec((None, D_EXPERT, D_MODEL), lambda b, be, nu, sl: (be[_blk(b, nu)], 0, 0)),
            ],
            out_specs=pl.BlockSpec(memory_space=pl.ANY),
            scratch_shapes=[pltpu.VMEM((2, MOE_BLK * ROW_SLABS, LANES), F32), pltpu.SemaphoreType.DMA((2,)),
                            pltpu.VMEM((D_EXPERT, D_MODEL), BF16)],
        ),
        out_shape=jax.ShapeDtypeStruct((2 * t_pad * ROW_SLABS, LANES), F32),
        compiler_params=_cparams(("arbitrary",), VMEM_LIMIT),
        name="moe_down",
    )(blk_e, n_used, slot, hid, w_down)


def _combine_body(x1_ref, ya_ref, yb_ref, gate_ref, g_ref, out_ref):
    gt = gate_ref[...]
    rows = x1_ref.shape[0]
    ya = _load_row_slabs(ya_ref, rows)
    yb = _load_row_slabs(yb_ref, rows)
    x2 = x1_ref[...] + (ya * gt[:, 0:1] + yb * gt[:, 1:2])
    out_ref[...] = _rms_scale(x2) * g_ref[...]


def _combine_kernel(x1_ref, ya_ref, yb_ref, gate_ref, g_ref, outp_ref, outs_ref, *, n_prompt_blocks):
    i = pl.program_id(0)

    @pl.when(i < n_prompt_blocks)
    def _():
        _combine_body(x1_ref, ya_ref, yb_ref, gate_ref, g_ref, outp_ref)

    @pl.when(i >= n_prompt_blocks)
    def _():
        _combine_body(x1_ref, ya_ref, yb_ref, gate_ref, g_ref, outs_ref)


def _combine(x1, gates_t, g_final, y2, tp, ts, t_pad):
    tm = ROW_TILE
    npb, nsb = tp // tm, ts // tm
    second = t_pad // tm
    return pl.pallas_call(
        functools.partial(_combine_kernel, n_prompt_blocks=npb),
        grid=(npb + nsb,),
        in_specs=[
            pl.BlockSpec((tm, D_MODEL), lambda i: (i, 0)),
            pl.BlockSpec((tm * ROW_SLABS, LANES), lambda i: (i, 0)),
            pl.BlockSpec((tm * ROW_SLABS, LANES), lambda i: (second + i, 0)),
            pl.BlockSpec((tm, 2), lambda i: (i, 0)),
            pl.BlockSpec((1, D_MODEL), lambda i: (0, 0)),
        ],
        out_specs=[
            pl.BlockSpec((tm, D_MODEL), lambda i: (jnp.minimum(i, npb - 1), 0)),
            pl.BlockSpec((tm, D_MODEL), lambda i: (jnp.maximum(i - npb, 0), 0)),
        ],
        out_shape=[jax.ShapeDtypeStruct((tp, D_MODEL), F32), jax.ShapeDtypeStruct((ts, D_MODEL), F32)],
        compiler_params=_cparams(("arbitrary",), VMEM_LIMIT),
        name="combine_final_norm",
    )(x1, y2, y2, gates_t, g_final.reshape(1, D_MODEL))


def _layer(xp, xs, g_attn, w_in, g_mix, sink_a, rpb_b, w_out, g_ffn, w_rg, w_re, w_gate, w_up, w_down, g_final):
    tp, ts = xp.shape[0], xs.shape[0]
    t_all = tp + ts
    bounds = (0, tp, t_all)
    s0, s1, s2, s3, s4 = QA_W, QA_W + KVA_W, QA_W + 2 * KVA_W, QA_W + 2 * KVA_W + B_W, QA_W + 2 * KVA_W + 2 * B_W
    w_in_p = jnp.concatenate([w_in[:, :s0], w_in[:, s2:s3], w_in[:, s3:s4], w_in[:, s4:], w_in[:, s0:s1], w_in[:, s1:s2]],
                             axis=1).astype(BF16)
    w_router = jnp.concatenate([w_re.T, w_rg.T, jnp.zeros((ROUTER_ROWS - N_EXPERTS - N_GROUPS, D_MODEL), F32)], axis=0)

    proj = _in_proj(xp, xs, g_attn, w_in_p)
    o_a = _window_attention(proj, sink_a, bounds)
    o_b = _neighborhood_attention(proj, _bias_table(rpb_b), bounds)
    x1, h2, logits_t = _out_proj(xp, xs, o_a, o_b, g_mix, w_out.astype(BF16), g_ffn, w_router)

    n_slots = 2 * t_all + N_EXPERTS * MOE_BLK
    n_blk = n_slots // MOE_BLK
    t_pad = t_all + 2 * MOE_BLK
    dest, gate, blk, pad = _route(logits_t, n_slots)
    slot = _slot_map(dest[0], dest[1], pad[:N_EXPERTS, 0], pad[N_EXPERTS:, 0], n_slots, t_pad)
    y2 = _moe(h2, slot, blk[0, :n_blk], blk[1, :1], w_gate, w_up, w_down, t_pad)
    return _combine(x1, gate[:2].T, g_final, y2, tp, ts, t_pad)


def kernel(x_prompt, x_sample, g_attn, w_in, g_mix, sink_a, rpb_b, w_out, g_ffn, w_router_group, w_router_expert,
           w_gate, w_up, w_down, g_final):
    assert x_prompt.shape[0] == 1 and x_sample.shape[0] == 1 and g_attn.shape[0] == 1
    yp, ys = _layer(x_prompt[0], x_sample[0], g_attn[0], w_in[0], g_mix[0], sink_a[0], rpb_b[0], w_out[0], g_ffn[0],
                    w_router_group[0], w_router_expert[0], w_gate[0], w_up[0], w_down[0], g_final)
    return yp[None], ys[None]
```

```python
import functools

import jax
import jax.numpy as jnp
from jax import lax
from jax.experimental import pallas as pl
from jax.experimental.pallas import tpu as pltpu

F32 = jnp.float32
BF16 = jnp.bfloat16
I32 = jnp.int32

D_MODEL = 2048
HEAD_DIM = 64
N_HEADS_A = 16
N_KV_A = 4
N_HEADS_B = 16
WIN = 128
GRID_W = 64
NA_ROWS = 8
NA_COLS = 16
N_GROUPS = 4
EPG = 8
N_EXPERTS = 32
D_EXPERT = 1024
EPS = 1e-6

QA_W = N_HEADS_A * HEAD_DIM
KVA_W = N_KV_A * HEAD_DIM
B_W = N_HEADS_B * HEAD_DIM
D_IN = QA_W + 2 * KVA_W + 3 * B_W
HEAD_GROUP_W = 256
Q_SCALE = HEAD_DIM ** -0.5

ROW_TILE = 512
NA_ROWS_PER_STEP = 8
MOE_BLK = 512
ROUTER_ROWS = 40
MOE_ROW_STEP = 128
VMEM_LIMIT = 56 * 1024 * 1024
MOE_VMEM_LIMIT = 60 * 1024 * 1024


def _cparams(sem, vmem=None):
    return pltpu.CompilerParams(dimension_semantics=sem, vmem_limit_bytes=vmem)


def _rms_scale(x):
    return x * lax.rsqrt(jnp.mean(x * x, axis=-1, keepdims=True) + EPS)


def _in_proj_body(x_ref, g_ref, w_ref, o_ref):
    h = (_rms_scale(x_ref[...]) * g_ref[...]).astype(BF16)
    nch = 512
    for j in range(D_IN // nch):
        sl = slice(j * nch, (j + 1) * nch)
        o_ref[:, sl] = jnp.dot(h, w_ref[:, sl], preferred_element_type=F32).astype(BF16)


def _in_proj_kernel(xp_ref, xs_ref, g_ref, w_ref, o_ref, *, n_prompt_blocks):
    i = pl.program_id(0)

    @pl.when(i < n_prompt_blocks)
    def _():
        _in_proj_body(xp_ref, g_ref, w_ref, o_ref)

    @pl.when(i >= n_prompt_blocks)
    def _():
        _in_proj_body(xs_ref, g_ref, w_ref, o_ref)


def _in_proj(xp, xs, g_attn, w_in_bf16):
    tp, ts = xp.shape[0], xs.shape[0]
    npb, nsb = tp // ROW_TILE, ts // ROW_TILE
    return pl.pallas_call(
        functools.partial(_in_proj_kernel, n_prompt_blocks=npb),
        grid=(npb + nsb,),
        in_specs=[
            pl.BlockSpec((ROW_TILE, D_MODEL), lambda i: (jnp.minimum(i, npb - 1), 0)),
            pl.BlockSpec((ROW_TILE, D_MODEL), lambda i: (jnp.maximum(i - npb, 0), 0)),
            pl.BlockSpec((1, D_MODEL), lambda i: (0, 0)),
            pl.BlockSpec((D_MODEL, D_IN), lambda i: (0, 0), pipeline_mode=pl.Buffered(1)),
        ],
        out_specs=pl.BlockSpec((ROW_TILE, D_IN), lambda i: (i, 0)),
        out_shape=jax.ShapeDtypeStruct((tp + ts, D_IN), BF16),
        compiler_params=_cparams(("arbitrary",), VMEM_LIMIT),
        name="in_proj",
    )(xp, xs, g_attn.reshape(1, D_MODEL), w_in_bf16)


def _seq_bounds(pos, bounds):
    lo = jnp.int32(bounds[0])
    hi = jnp.int32(bounds[1])
    for k in range(1, len(bounds) - 1):
        inside = pos >= bounds[k]
        lo = jnp.where(inside, bounds[k], lo)
        hi = jnp.where(inside, bounds[k + 1], hi)
    return lo, hi


def _lane_head(width):
    return lax.broadcasted_iota(I32, (1, width), 1) // HEAD_DIM


def _stack_heads(x, lane_head):
    zero = jnp.zeros_like(x)
    return jnp.concatenate([jnp.where(lane_head == r, x, zero) for r in range(4)], axis=0)


def _diag_heads(o_all, m, lane_head):
    out = jnp.where(lane_head == 0, o_all[0:m], 0.0)
    for r in range(1, 4):
        out = out + jnp.where(lane_head == r, o_all[r * m:(r + 1) * m], 0.0)
    return out


_NT = (((1,), (1,)), ((), ()))


def _win_kernel(sink_ref, q_ref, kp_ref, km_ref, kn_ref, vp_ref, vm_ref, vn_ref, o_ref,
                kbuf, vbuf, bias_scr, *, bounds):
    i = pl.program_id(0)
    tq = q_ref.shape[0]
    n_sub = tq // WIN
    tok0 = i * tq
    lo, hi = _seq_bounds(tok0, bounds)
    lane_head = _lane_head(HEAD_GROUP_W)

    @pl.when(i == 0)
    def _():
        qi = lax.broadcasted_iota(I32, (WIN, 3 * WIN), 0)
        ki = lax.broadcasted_iota(I32, (WIN, 3 * WIN), 1)
        dist_i = jnp.abs(qi - ki + WIN)
        dist = dist_i.astype(F32)
        for h in range(N_HEADS_A):
            slope = 2.0 ** (-8.0 * (h + 1.0) / N_HEADS_A)
            base = jnp.where(dist_i <= WIN, -(slope * dist), -jnp.inf)
            bias_scr[0, h] = base
            bias_scr[1, h] = jnp.where(ki >= WIN, base, -jnp.inf)
            bias_scr[2, h] = jnp.where(ki < 2 * WIN, base, -jnp.inf)

    er = lax.broadcasted_iota(I32, (HEAD_GROUP_W, HEAD_GROUP_W), 0)
    ec = lax.broadcasted_iota(I32, (HEAD_GROUP_W, HEAD_GROUP_W), 1)
    for g in range(N_KV_A):
        spread = jnp.where((er // HEAD_DIM == g) & (er % HEAD_DIM == ec % HEAD_DIM), 1.0, 0.0).astype(BF16)
        for buf, (p_ref, m_ref, n_ref) in ((kbuf, (kp_ref, km_ref, kn_ref)), (vbuf, (vp_ref, vm_ref, vn_ref))):
            buf[g, 0:WIN, :] = jnp.dot(p_ref[...], spread, preferred_element_type=F32).astype(BF16)
            buf[g, WIN:WIN + tq, :] = jnp.dot(m_ref[...], spread, preferred_element_type=F32).astype(BF16)
            buf[g, WIN + tq:, :] = jnp.dot(n_ref[...], spread, preferred_element_type=F32).astype(BF16)

    for g in range(N_KV_A):
        for c in range(n_sub):
            r0 = c * WIN
            qg = q_ref[r0:r0 + WIN, g * HEAD_GROUP_W:(g + 1) * HEAD_GROUP_W] * Q_SCALE
            lhs = _stack_heads(qg, lane_head)
            kwin = kbuf[g, r0:r0 + 3 * WIN, :]
            vwin = vbuf[g, r0:r0 + 3 * WIN, :]
            s = lax.dot_general(lhs, kwin, _NT, preferred_element_type=F32)
            variant = 0
            if c == 0:
                variant = jnp.where(tok0 == lo, 1, 0)
            if c == n_sub - 1:
                variant = jnp.where(tok0 + tq == hi, 2, 0)
            probs = []
            for r in range(4):
                h = g * 4 + r
                sink = sink_ref[h]
                t = s[r * WIN:(r + 1) * WIN] + bias_scr[variant, h]
                m = jnp.maximum(jnp.max(t, axis=-1, keepdims=True), sink)
                p = jnp.exp(t - m)
                den = jnp.sum(p, axis=-1, keepdims=True) + jnp.exp(sink - m)
                probs.append((p * (1.0 / den)).astype(BF16))
            p_all = jnp.concatenate(probs, axis=0)
            o_all = jnp.dot(p_all, vwin, preferred_element_type=F32)
            o_ref[r0:r0 + WIN, g * HEAD_GROUP_W:(g + 1) * HEAD_GROUP_W] = _diag_heads(o_all, WIN, lane_head)


def _window_attention(proj, sink, bounds):
    t_all = proj.shape[0]
    tq = ROW_TILE
    nblk = t_all // tq
    halo_per_tile = tq // WIN
    n_halo = t_all // WIN
    k_col = (QA_W + 3 * B_W) // KVA_W
    v_col = k_col + 1

    def main_spec(col):
        return pl.BlockSpec((tq, KVA_W), lambda i: (i, col))

    def prev_spec(col):
        return pl.BlockSpec((WIN, KVA_W), lambda i: (jnp.maximum(i * halo_per_tile - 1, 0), col))

    def next_spec(col):
        return pl.BlockSpec((WIN, KVA_W), lambda i: (jnp.minimum((i + 1) * halo_per_tile, n_halo - 1), col))

    return pl.pallas_call(
        functools.partial(_win_kernel, bounds=bounds),
        grid=(nblk,),
        in_specs=[
            pl.BlockSpec(memory_space=pltpu.SMEM),
            pl.BlockSpec((tq, QA_W), lambda i: (i, 0)),
            prev_spec(k_col), main_spec(k_col), next_spec(k_col),
            prev_spec(v_col), main_spec(v_col), next_spec(v_col),
        ],
        out_specs=pl.BlockSpec((tq, QA_W), lambda i: (i, 0)),
        out_shape=jax.ShapeDtypeStruct((t_all, QA_W), F32),
        scratch_shapes=[pltpu.VMEM((N_KV_A, tq + 2 * WIN, KVA_W), BF16), pltpu.VMEM((N_KV_A, tq + 2 * WIN, KVA_W), BF16),
                        pltpu.VMEM((3, N_HEADS_A, WIN, 3 * WIN), F32)],
        compiler_params=_cparams(("arbitrary",), VMEM_LIMIT),
        name="window_attention",
    )(sink, proj, proj, proj, proj, proj, proj, proj)


def _bias_table_kernel(rpb_ref, o_ref):
    g = pl.program_id(0)
    n_rel_rows = 2 * NA_ROWS - 1
    n_rel_cols = 2 * NA_COLS - 1
    qc = lax.broadcasted_iota(I32, (GRID_W, 2 * GRID_W), 0)
    lane = lax.broadcasted_iota(I32, (GRID_W, 2 * GRID_W), 1)
    kc = lane % GRID_W
    second = lane >= GRID_W
    rel = jnp.clip(kc - qc + NA_COLS - 1, 0, n_rel_cols - 1)
    col_start = jnp.clip(qc - NA_COLS // 2, 0, GRID_W - NA_COLS)
    col_valid = (kc >= col_start) & (kc < col_start + NA_COLS)
    for r in range(4):
        h = g * 4 + r
        for j in range(n_rel_rows - 1):
            base_a = (h * n_rel_rows + j) * n_rel_cols
            base_b = base_a + n_rel_cols
            tile = jnp.zeros((GRID_W, 2 * GRID_W), F32)
            for d in range(n_rel_cols):
                val = jnp.where(second, rpb_ref[base_b + d], rpb_ref[base_a + d])
                tile = jnp.where(rel == d, val, tile)
            tile = jnp.where(col_valid, tile, -jnp.inf)
            for j0 in range(NA_ROWS):
                k = j - j0
                if k >= 0 and k % 2 == 0 and k < NA_ROWS:
                    o_ref[j0, r * GRID_W:(r + 1) * GRID_W, k * GRID_W:(k + 2) * GRID_W] = tile


def _bias_table(rpb):
    return pl.pallas_call(
        _bias_table_kernel,
        grid=(4,),
        in_specs=[pl.BlockSpec(memory_space=pltpu.SMEM)],
        out_specs=pl.BlockSpec((NA_ROWS, None, 4 * GRID_W, NA_ROWS * GRID_W), lambda g: (0, g, 0, 0)),
        out_shape=jax.ShapeDtypeStruct((NA_ROWS, 4, 4 * GRID_W, NA_ROWS * GRID_W), F32),
        compiler_params=_cparams(("arbitrary",)),
        name="na_bias_table",
    )(rpb.reshape(-1))


def _na_kernel(q_ref, kp_ref, km_ref, kn_ref, vp_ref, vm_ref, vn_ref, bt_ref, o_ref, kbuf, vbuf, *, row_bounds):
    i = pl.program_id(1)
    rows = NA_ROWS_PER_STEP
    tile = rows * GRID_W
    lane_head = _lane_head(HEAD_GROUP_W)
    for buf, (p_ref, m_ref, n_ref) in ((kbuf, (kp_ref, km_ref, kn_ref)), (vbuf, (vp_ref, vm_ref, vn_ref))):
        buf[0:tile, :] = p_ref[...]
        buf[tile:2 * tile, :] = m_ref[...]
        buf[2 * tile:3 * tile, :] = n_ref[...]
    row0 = i * rows
    lo, hi = _seq_bounds(row0, row_bounds)

    for jr in range(rows):
        r = row0 + jr
        rs = jnp.clip(r - NA_ROWS // 2, lo, hi - NA_ROWS)
        j0 = rs - r + NA_ROWS - 1
        off = pl.multiple_of((rs - row0 + rows) * GRID_W, GRID_W)
        q0 = jr * GRID_W
        q = q_ref[q0:q0 + GRID_W, :] * Q_SCALE
        lhs = _stack_heads(q, lane_head)
        kwin = kbuf[pl.ds(off, NA_ROWS * GRID_W), :]
        vwin = vbuf[pl.ds(off, NA_ROWS * GRID_W), :]
        s = lax.dot_general(lhs, kwin, _NT, preferred_element_type=F32) + bt_ref[j0]
        m = jnp.max(s, axis=-1, keepdims=True)
        p = jnp.exp(s - m)
        den = jnp.sum(p, axis=-1, keepdims=True)
        pb = (p * (1.0 / den)).astype(BF16)
        o_all = jnp.dot(pb, vwin, preferred_element_type=F32)
        o_ref[q0:q0 + GRID_W, :] = _diag_heads(o_all, GRID_W, lane_head)


def _neighborhood_attention(proj, bias_table, bounds):
    t_all = proj.shape[0]
    tile = NA_ROWS_PER_STEP * GRID_W
    nblk = t_all // tile
    row_bounds = tuple(b // GRID_W for b in bounds)
    q_col = QA_W // HEAD_GROUP_W
    k_col = (QA_W + B_W) // HEAD_GROUP_W
    v_col = (QA_W + 2 * B_W) // HEAD_GROUP_W

    def spec(col, shift):
        return pl.BlockSpec((tile, HEAD_GROUP_W), lambda g, i: (jnp.clip(i + shift, 0, nblk - 1), col + g))

    return pl.pallas_call(
        functools.partial(_na_kernel, row_bounds=row_bounds),
        grid=(4, nblk),
        in_specs=[
            spec(q_col, 0),
            spec(k_col, -1), spec(k_col, 0), spec(k_col, 1),
            spec(v_col, -1), spec(v_col, 0), spec(v_col, 1),
            pl.BlockSpec((NA_ROWS, None, 4 * GRID_W, NA_ROWS * GRID_W), lambda g, i: (0, g, 0, 0)),
        ],
        out_specs=pl.BlockSpec((tile, HEAD_GROUP_W), lambda g, i: (i, g)),
        out_shape=jax.ShapeDtypeStruct((t_all, B_W), F32),
        scratch_shapes=[pltpu.VMEM((3 * tile, HEAD_GROUP_W), BF16), pltpu.VMEM((3 * tile, HEAD_GROUP_W), BF16)],
        compiler_params=_cparams(("arbitrary", "arbitrary"), VMEM_LIMIT),
        name="neighborhood_attention",
    )(proj, proj, proj, proj, proj, proj, proj, bias_table)


def _split_bf16(x):
    hi = x.astype(BF16)
    lo = (x - hi.astype(F32)).astype(BF16)
    return hi, lo


def _out_proj_body(x_ref, oa_ref, ob_ref, gmix_ref, wout_ref, gffn_ref, wr_ref, x1_ref, h2_ref, lg_ref):
    ma = (_rms_scale(oa_ref[...]) * gmix_ref[:, :QA_W]).astype(BF16)
    mb = (_rms_scale(ob_ref[...]) * gmix_ref[:, QA_W:]).astype(BF16)
    y = jnp.dot(ma, wout_ref[:QA_W, :], preferred_element_type=F32)
    y = y + jnp.dot(mb, wout_ref[QA_W:, :], preferred_element_type=F32)
    x1 = x_ref[...] + y
    x1_ref[...] = x1
    h2 = _rms_scale(x1) * gffn_ref[...]
    h2_ref[...] = h2
    h_hi, h_lo = _split_bf16(h2)
    w_hi, w_lo = _split_bf16(wr_ref[...])
    lg = lax.dot_general(w_hi, h_hi, _NT, preferred_element_type=F32)
    lg = lg + lax.dot_general(w_lo, h_hi, _NT, preferred_element_type=F32)
    lg = lg + lax.dot_general(w_hi, h_lo, _NT, preferred_element_type=F32)
    lg_ref[...] = lg


def _out_proj_kernel(xp_ref, xs_ref, *rest, n_prompt_blocks):
    i = pl.program_id(0)

    @pl.when(i < n_prompt_blocks)
    def _():
        _out_proj_body(xp_ref, *rest)

    @pl.when(i >= n_prompt_blocks)
    def _():
        _out_proj_body(xs_ref, *rest)


def _out_proj(xp, xs, o_a, o_b, g_mix, w_out_bf16, g_ffn, w_router):
    tp, ts = xp.shape[0], xs.shape[0]
    t_all = tp + ts
    npb, nsb = tp // ROW_TILE, ts // ROW_TILE
    row = lambda i: (i, 0)
    const = lambda i: (0, 0)
    return pl.pallas_call(
        functools.partial(_out_proj_kernel, n_prompt_blocks=npb),
        grid=(npb + nsb,),
        in_specs=[
            pl.BlockSpec((ROW_TILE, D_MODEL), lambda i: (jnp.minimum(i, npb - 1), 0)),
            pl.BlockSpec((ROW_TILE, D_MODEL), lambda i: (jnp.maximum(i - npb, 0), 0)),
            pl.BlockSpec((ROW_TILE, QA_W), row),
            pl.BlockSpec((ROW_TILE, B_W), row),
            pl.BlockSpec((1, D_MODEL), const),
            pl.BlockSpec((D_MODEL, D_MODEL), const, pipeline_mode=pl.Buffered(1)),
            pl.BlockSpec((1, D_MODEL), const),
            pl.BlockSpec((ROUTER_ROWS, D_MODEL), const),
        ],
        out_specs=[
            pl.BlockSpec((ROW_TILE, D_MODEL), row),
            pl.BlockSpec((ROW_TILE, D_MODEL), row),
            pl.BlockSpec((ROUTER_ROWS, ROW_TILE), lambda i: (0, i)),
        ],
        out_shape=[
            jax.ShapeDtypeStruct((t_all, D_MODEL), F32),
            jax.ShapeDtypeStruct((t_all, D_MODEL), F32),
            jax.ShapeDtypeStruct((ROUTER_ROWS, t_all), F32),
        ],
        compiler_params=_cparams(("arbitrary",), VMEM_LIMIT),
        name="out_proj_router",
    )(xp, xs, o_a, o_b, g_mix.reshape(1, D_MODEL), w_out_bf16, g_ffn.reshape(1, D_MODEL), w_router)


def _route_kernel(l_ref, dest_ref, gate_ref, blk_ref, pref_scr, *, t_all, n_blk):
    chunk = 256
    lanes = 128
    sub8 = lax.broadcasted_iota(I32, (EPG, t_all), 0)
    gl = jnp.where(sub8 < N_GROUPS, l_ref[N_EXPERTS:N_EXPERTS + EPG, :], -jnp.inf)
    gmax = jnp.max(gl, axis=0, keepdims=True)
    gidx = jnp.min(jnp.where(gl == gmax, sub8, EPG), axis=0, keepdims=True)
    p_grp = 1.0 / jnp.sum(jnp.exp(gl - gmax), axis=0, keepdims=True)

    e_sel = l_ref[(N_GROUPS - 1) * EPG:N_GROUPS * EPG, :]
    for g in range(N_GROUPS - 2, -1, -1):
        e_sel = jnp.where(gidx == g, l_ref[g * EPG:(g + 1) * EPG, :], e_sel)
    ee = jnp.exp(e_sel - jnp.max(e_sel, axis=0, keepdims=True))
    pe = ee * (1.0 / jnp.sum(ee, axis=0, keepdims=True))
    v1 = jnp.max(pe, axis=0, keepdims=True)
    i1 = jnp.min(jnp.where(pe == v1, sub8, EPG), axis=0, keepdims=True)
    pe2 = jnp.where(sub8 == i1, -1.0, pe)
    v2 = jnp.max(pe2, axis=0, keepdims=True)
    i2 = jnp.min(jnp.where(pe2 == v2, sub8, EPG), axis=0, keepdims=True)
    norm = p_grp * (1.0 / (v1 + v2))
    gate_ref[...] = jnp.where(sub8 == 0, v1 * norm, jnp.where(sub8 == 1, v2 * norm, 0.0))
    e1 = gidx * EPG + i1
    e2 = gidx * EPG + i2

    tri = jnp.where(lax.broadcasted_iota(I32, (chunk, chunk), 0) < lax.broadcasted_iota(I32, (chunk, chunk), 1),
                    1.0, 0.0).astype(BF16)
    sub32 = lax.broadcasted_iota(I32, (N_EXPERTS, chunk), 0)
    count = jnp.zeros((N_EXPERTS, 1), F32)
    for c in range(t_all // chunk):
        sl = slice(c * chunk, (c + 1) * chunk)
        member = jnp.where((sub32 == e1[:, sl]) | (sub32 == e2[:, sl]), 1.0, 0.0)
        pref_scr[:, sl] = jnp.dot(member.astype(BF16), tri, preferred_element_type=F32) + count
        count = count + jnp.sum(member, axis=1, keepdims=True)

    nblk_e = jnp.floor((count + (MOE_BLK - 1)) * (1.0 / MOE_BLK))
    nb_l = jnp.broadcast_to(nblk_e, (N_EXPERTS, lanes))
    hi16 = jnp.floor(nb_l * (1.0 / 16.0))
    lo16 = nb_l - 16.0 * hi16
    ltri = jnp.where(lax.broadcasted_iota(I32, (N_EXPERTS, N_EXPERTS), 1) < lax.broadcasted_iota(I32, (N_EXPERTS, N_EXPERTS), 0),
                     1.0, 0.0).astype(BF16)
    first_blk = (16.0 * jnp.dot(ltri, hi16.astype(BF16), preferred_element_type=F32)
                 + jnp.dot(ltri, lo16.astype(BF16), preferred_element_type=F32))
    end_blk = first_blk + nb_l
    pstart = first_blk[:, 0:1] * float(MOE_BLK)

    sub32t = lax.broadcasted_iota(I32, (N_EXPERTS, t_all), 0)
    slot = pref_scr[...] + pstart
    d1 = jnp.sum(jnp.where(sub32t == e1, slot, 0.0), axis=0, keepdims=True)
    d2 = jnp.sum(jnp.where(sub32t == e2, slot, 0.0), axis=0, keepdims=True)
    dest_ref[...] = jnp.where(sub8 == 0, d1, jnp.where(sub8 == 1, d2, 0.0)).astype(I32)

    blk_i = lax.broadcasted_iota(I32, (N_EXPERTS, n_blk), 1).astype(F32)
    owner = jnp.sum(jnp.where(end_blk[:, 0:1] <= blk_i, 1.0, 0.0), axis=0, keepdims=True)
    owner = jnp.minimum(owner, float(N_EXPERTS - 1))
    n_used = jnp.sum(nblk_e, axis=0, keepdims=True)
    inside = (first_blk[:, 0:1] <= blk_i) & (blk_i < end_blk[:, 0:1])
    left = count - (blk_i - first_blk[:, 0:1]) * float(MOE_BLK)
    rows = jnp.sum(jnp.where(inside, jnp.minimum(left, float(MOE_BLK)), 0.0), axis=0, keepdims=True)
    row8 = lax.broadcasted_iota(I32, (8, n_blk), 0)
    blk_ref[...] = jnp.where(row8 == 0, owner, jnp.where(row8 == 1, n_used, jnp.where(row8 == 2, rows, 0.0))).astype(I32)


def _route(logits_t, n_slots):
    t_all = logits_t.shape[1]
    n_blk = pl.cdiv(n_slots // MOE_BLK, 128) * 128
    vmem = pl.BlockSpec(memory_space=pltpu.VMEM)
    return pl.pallas_call(
        functools.partial(_route_kernel, t_all=t_all, n_blk=n_blk),
        in_specs=[vmem],
        out_specs=[vmem, vmem, vmem],
        out_shape=[
            jax.ShapeDtypeStruct((8, t_all), I32),
            jax.ShapeDtypeStruct((8, t_all), F32),
            jax.ShapeDtypeStruct((8, n_blk), I32),
        ],
        scratch_shapes=[pltpu.VMEM((N_EXPERTS, t_all), F32)],
        compiler_params=pltpu.CompilerParams(vmem_limit_bytes=VMEM_LIMIT),
        name="route",
    )(logits_t)


def _slot_map_kernel(d1_ref, d2_ref, init_hbm, slot_ref, sem, *, t_all, t_pad):
    fill = pltpu.make_async_copy(init_hbm, slot_ref, sem)
    fill.start()
    fill.wait()
    unroll = 8

    def body(j, c):
        for u in range(unroll):
            t = j * unroll + u
            slot_ref[d1_ref[t]] = t
            slot_ref[d2_ref[t]] = t_pad + t
        return c

    lax.fori_loop(0, t_all // unroll, body, 0)


def _slot_map(d1, d2, n_slots, t_pad):
    t_all = d1.shape[0]
    init = t_all + jnp.bitwise_and(jnp.arange(n_slots, dtype=I32), 2 * MOE_BLK - 1)
    smem = pl.BlockSpec(memory_space=pltpu.SMEM)
    return pl.pallas_call(
        functools.partial(_slot_map_kernel, t_all=t_all, t_pad=t_pad),
        in_specs=[smem, smem, pl.BlockSpec(memory_space=pl.ANY)],
        out_specs=smem,
        out_shape=jax.ShapeDtypeStruct((n_slots,), I32),
        scratch_shapes=[pltpu.SemaphoreType.DMA(())],
        name="slot_map",
    )(d1, d2, init)


def _expert_changed(b, blk_e_ref):
    prev = blk_e_ref[jnp.maximum(b - 1, 0)]
    return (b == 0) | (blk_e_ref[b] != prev)


def _gather_start(slot_ref, h_hbm, xbuf, sem, blk, buf, r, t_all, t_pad):
    row = slot_ref[blk * MOE_BLK + r]
    tok = jnp.minimum(jnp.where(row >= t_pad, row - t_pad, row), t_all - 1)
    pltpu.make_async_copy(h_hbm.at[pl.ds(tok, 1), :], xbuf.at[buf, pl.ds(r, 1), :], sem.at[buf]).start(priority=r % 2)


def _gather_wait(h_hbm, xbuf, sem, buf):
    for r in range(MOE_BLK):
        pltpu.make_async_copy(h_hbm.at[pl.ds(0, 1), :], xbuf.at[buf, pl.ds(r, 1), :], sem.at[buf]).wait()


def _row_steps(rows_ref, b):
    return jnp.right_shift(rows_ref[b] + (MOE_ROW_STEP - 1), MOE_ROW_STEP.bit_length() - 1)


def _moe_up_kernel(blk_e_ref, nused_ref, rows_ref, slot_ref, h_hbm, wg_ref, wu_ref, hid_ref, xbuf, sem, wg_bf, wu_bf,
                   *, t_all, t_pad):
    b = pl.program_id(0)
    nused = nused_ref[0]
    cur = lax.rem(b, 2)
    n_chunks = 4
    rows_per_chunk = MOE_BLK // n_chunks
    nch = D_EXPERT // n_chunks
    steps = _row_steps(rows_ref, b)

    @pl.when(b == 0)
    def _():
        for r in range(MOE_BLK):
            _gather_start(slot_ref, h_hbm, xbuf, sem, 0, 0, r, t_all, t_pad)

    @pl.when(b < nused)
    def _():
        _gather_wait(h_hbm, xbuf, sem, cur)

        @pl.when(_expert_changed(b, blk_e_ref))
        def _():
            wg_bf[...] = wg_ref[...].astype(BF16)
            wu_bf[...] = wu_ref[...].astype(BF16)

    def compute(m):
        nxt = jnp.minimum(b + 1, nused - 1)
        x = xbuf[cur, 0:m, :].astype(BF16)
        for j in range(n_chunks):
            for r in range(j * rows_per_chunk, (j + 1) * rows_per_chunk):
                _gather_start(slot_ref, h_hbm, xbuf, sem, nxt, 1 - cur, r, t_all, t_pad)
            sl = slice(j * nch, (j + 1) * nch)
            gt = jnp.dot(x, wg_bf[:, sl], preferred_element_type=F32)
            up = jnp.dot(x, wu_bf[:, sl], preferred_element_type=F32)
            hid_ref[0:m, sl] = (gt * (1.0 / (1.0 + jnp.exp(-gt))) * up).astype(BF16)
        if m < MOE_BLK:
            hid_ref[m:, :] = jnp.zeros((MOE_BLK - m, D_EXPERT), BF16)

    for k in range(1, MOE_BLK // MOE_ROW_STEP + 1):
        @pl.when((b < nused) & (steps == k))
        def _(k=k):
            compute(k * MOE_ROW_STEP)

    @pl.when(b == nused - 1)
    def _():
        _gather_wait(h_hbm, xbuf, sem, 1 - cur)

    @pl.when(b >= nused)
    def _():
        hid_ref[...] = jnp.zeros_like(hid_ref)


def _scatter_start(slot_ref, ybuf, y_hbm, sem, blk, buf, r):
    row = slot_ref[blk * MOE_BLK + r]
    pltpu.make_async_copy(ybuf.at[buf, pl.ds(r, 1), :], y_hbm.at[pl.ds(row, 1), :], sem.at[buf]).start()


def _scatter_wait(ybuf, y_hbm, sem, buf):
    for r in range(MOE_BLK):
        pltpu.make_async_copy(ybuf.at[buf, pl.ds(r, 1), :], y_hbm.at[pl.ds(0, 1), :], sem.at[buf]).wait()


def _moe_down_kernel(blk_e_ref, nused_ref, rows_ref, slot_ref, hid_ref, wd_ref, y_hbm, ybuf, sem, wd_bf, *, t_all, t_pad):
    b = pl.program_id(0)
    nused = nused_ref[0]
    cur = lax.rem(b, 2)
    n_chunks = 4
    rows_per_chunk = MOE_BLK // n_chunks
    nch = D_MODEL // n_chunks
    steps = _row_steps(rows_ref, b)

    @pl.when(b == 0)
    def _():
        ybuf[...] = jnp.zeros(ybuf.shape, F32)
        for half in range(2):
            for part in range(2):
                row0 = half * t_pad + t_all + part * MOE_BLK
                cp = pltpu.make_async_copy(ybuf.at[0], y_hbm.at[pl.ds(row0, MOE_BLK), :], sem.at[0])
                cp.start()
                cp.wait()

    @pl.when((b >= 2) & (b <= nused))
    def _():
        _scatter_wait(ybuf, y_hbm, sem, cur)

    def compute(m, scatter_previous):
        h = hid_ref[0:m, :]
        for j in range(n_chunks):
            if scatter_previous:
                for r in range(j * rows_per_chunk, (j + 1) * rows_per_chunk):
                    _scatter_start(slot_ref, ybuf, y_hbm, sem, b - 1, 1 - cur, r)
            sl = slice(j * nch, (j + 1) * nch)
            ybuf[cur, 0:m, sl] = jnp.dot(h, wd_bf[:, sl], preferred_element_type=F32)

    @pl.when(b == 0)
    def _():
        wd_bf[...] = wd_ref[...].astype(BF16)
        compute(MOE_BLK, False)

    @pl.when((b >= 1) & (b < nused) & _expert_changed(b, blk_e_ref))
    def _():
        wd_bf[...] = wd_ref[...].astype(BF16)

    for k in range(1, MOE_BLK // MOE_ROW_STEP + 1):
        @pl.when((b >= 1) & (b < nused) & (steps == k))
        def _(k=k):
            compute(k * MOE_ROW_STEP, True)

    @pl.when(b == nused)
    def _():
        for r in range(MOE_BLK):
            _scatter_start(slot_ref, ybuf, y_hbm, sem, b - 1, 1 - cur, r)
        _scatter_wait(ybuf, y_hbm, sem, 1 - cur)


def _blk(b, nused_ref):
    return jnp.minimum(b, nused_ref[0] - 1)


def _moe(h2, slot, blk_e, n_used, blk_rows, w_gate, w_up, w_down, t_pad):
    t_all = h2.shape[0]
    n_slots = slot.shape[0]
    n_blk = n_slots // MOE_BLK
    hid = pl.pallas_call(
        functools.partial(_moe_up_kernel, t_all=t_all, t_pad=t_pad),
        grid_spec=pltpu.PrefetchScalarGridSpec(
            num_scalar_prefetch=4,
            grid=(n_blk,),
            in_specs=[
                pl.BlockSpec(memory_space=pl.ANY),
                pl.BlockSpec((None, D_MODEL, D_EXPERT), lambda b, be, nu, rw, sl: (be[_blk(b, nu)], 0, 0)),
                pl.BlockSpec((None, D_MODEL, D_EXPERT), lambda b, be, nu, rw, sl: (be[_blk(b, nu)], 0, 0)),
            ],
            out_specs=pl.BlockSpec((MOE_BLK, D_EXPERT), lambda b, be, nu, rw, sl: (b, 0)),
            scratch_shapes=[pltpu.VMEM((2, MOE_BLK, D_MODEL), F32), pltpu.SemaphoreType.DMA((2,)),
                            pltpu.VMEM((D_MODEL, D_EXPERT), BF16), pltpu.VMEM((D_MODEL, D_EXPERT), BF16)],
        ),
        out_shape=jax.ShapeDtypeStruct((n_slots, D_EXPERT), BF16),
        compiler_params=_cparams(("arbitrary",), MOE_VMEM_LIMIT),
        name="moe_gate_up",
    )(blk_e, n_used, blk_rows, slot, h2, w_gate, w_up)
    return pl.pallas_call(
        functools.partial(_moe_down_kernel, t_all=t_all, t_pad=t_pad),
        grid_spec=pltpu.PrefetchScalarGridSpec(
            num_scalar_prefetch=4,
            grid=(n_blk,),
            in_specs=[
                pl.BlockSpec((MOE_BLK, D_EXPERT), lambda b, be, nu, rw, sl: (_blk(b, nu), 0)),
                pl.BlockSpec((None, D_EXPERT, D_MODEL), lambda b, be, nu, rw, sl: (be[_blk(b, nu)], 0, 0)),
            ],
            out_specs=pl.BlockSpec(memory_space=pl.ANY),
            scratch_shapes=[pltpu.VMEM((2, MOE_BLK, D_MODEL), F32), pltpu.SemaphoreType.DMA((2,)),
                            pltpu.VMEM((D_EXPERT, D_MODEL), BF16)],
        ),
        out_shape=jax.ShapeDtypeStruct((2 * t_pad, D_MODEL), F32),
        compiler_params=_cparams(("arbitrary",), VMEM_LIMIT),
        name="moe_down",
    )(blk_e, n_used, blk_rows, slot, hid, w_down)


def _combine_body(x1_ref, ya_ref, yb_ref, gate_ref, g_ref, out_ref):
    gt = gate_ref[...]
    x2 = x1_ref[...] + (ya_ref[...] * gt[:, 0:1] + yb_ref[...] * gt[:, 1:2])
    out_ref[...] = _rms_scale(x2) * g_ref[...]


def _combine_kernel(x1_ref, ya_ref, yb_ref, gate_ref, g_ref, outp_ref, outs_ref, *, n_prompt_blocks):
    i = pl.program_id(0)

    @pl.when(i < n_prompt_blocks)
    def _():
        _combine_body(x1_ref, ya_ref, yb_ref, gate_ref, g_ref, outp_ref)

    @pl.when(i >= n_prompt_blocks)
    def _():
        _combine_body(x1_ref, ya_ref, yb_ref, gate_ref, g_ref, outs_ref)


def _combine(x1, gates_t, g_final, y2, tp, ts, t_pad):
    tm = ROW_TILE
    npb, nsb = tp // tm, ts // tm
    second = t_pad // tm
    return pl.pallas_call(
        functools.partial(_combine_kernel, n_prompt_blocks=npb),
        grid=(npb + nsb,),
        in_specs=[
            pl.BlockSpec((tm, D_MODEL), lambda i: (i, 0)),
            pl.BlockSpec((tm, D_MODEL), lambda i: (i, 0)),
            pl.BlockSpec((tm, D_MODEL), lambda i: (second + i, 0)),
            pl.BlockSpec((tm, 2), lambda i: (i, 0)),
            pl.BlockSpec((1, D_MODEL), lambda i: (0, 0)),
        ],
        out_specs=[
            pl.BlockSpec((tm, D_MODEL), lambda i: (jnp.minimum(i, npb - 1), 0)),
            pl.BlockSpec((tm, D_MODEL), lambda i: (jnp.maximum(i - npb, 0), 0)),
        ],
        out_shape=[jax.ShapeDtypeStruct((tp, D_MODEL), F32), jax.ShapeDtypeStruct((ts, D_MODEL), F32)],
        compiler_params=_cparams(("arbitrary",), VMEM_LIMIT),
        name="combine_final_norm",
    )(x1, y2, y2, gates_t, g_final.reshape(1, D_MODEL))


def _layer(xp, xs, g_attn, w_in, g_mix, sink_a, rpb_b, w_out, g_ffn, w_rg, w_re, w_gate, w_up, w_down, g_final):
    tp, ts = xp.shape[0], xs.shape[0]
    t_all = tp + ts
    bounds = (0, tp, t_all)
    s0, s1, s2, s3, s4 = QA_W, QA_W + KVA_W, QA_W + 2 * KVA_W, QA_W + 2 * KVA_W + B_W, QA_W + 2 * KVA_W + 2 * B_W
    w_in_p = jnp.concatenate([w_in[:, :s0], w_in[:, s2:s3], w_in[:, s3:s4], w_in[:, s4:], w_in[:, s0:s1], w_in[:, s1:s2]],
                             axis=1).astype(BF16)
    w_router = jnp.concatenate([w_re.T, w_rg.T, jnp.zeros((ROUTER_ROWS - N_EXPERTS - N_GROUPS, D_MODEL), F32)], axis=0)

    proj = _in_proj(xp, xs, g_attn, w_in_p)
    o_a = _window_attention(proj, sink_a, bounds)
    o_b = _neighborhood_attention(proj, _bias_table(rpb_b), bounds)
    x1, h2, logits_t = _out_proj(xp, xs, o_a, o_b, g_mix, w_out.astype(BF16), g_ffn, w_router)

    n_slots = 2 * t_all + N_EXPERTS * MOE_BLK
    n_blk = n_slots // MOE_BLK
    t_pad = t_all + 2 * MOE_BLK
    dest, gate, blk = _route(logits_t, n_slots)
    slot = _slot_map(dest[0], dest[1], n_slots, t_pad)
    y2 = _moe(h2, slot, blk[0, :n_blk], blk[1, :1], blk[2, :n_blk], w_gate, w_up, w_down, t_pad)
    return _combine(x1, gate[:2].T, g_final, y2, tp, ts, t_pad)


def kernel(x_prompt, x_sample, g_attn, w_in, g_mix, sink_a, rpb_b, w_out, g_ffn, w_router_group, w_router_expert,
           w_gate, w_up, w_down, g_final):
    assert x_prompt.shape[0] == 1 and x_sample.shape[0] == 1 and g_attn.shape[0] == 1
    yp, ys = _layer(x_prompt[0], x_sample[0], g_attn[0], w_in[0], g_mix[0], sink_a[0], rpb_b[0], w_out[0], g_ffn[0],
                    w_router_group[0], w_router_expert[0], w_gate[0], w_up[0], w_down[0], g_final)
    return yp[None], ys[None]
```

```python
import functools

import jax
import jax.numpy as jnp
from jax import lax
from jax.experimental import pallas as pl
from jax.experimental.pallas import tpu as pltpu

F32 = jnp.float32
BF16 = jnp.bfloat16
I32 = jnp.int32

D_MODEL = 2048
HEAD_DIM = 64
N_HEADS_A = 16
N_KV_A = 4
N_HEADS_B = 16
WIN = 128
GRID_W = 64
NA_ROWS = 8
NA_COLS = 16
N_GROUPS = 4
EPG = 8
N_EXPERTS = 32
D_EXPERT = 1024
EPS = 1e-6

QA_W = N_HEADS_A * HEAD_DIM
KVA_W = N_KV_A * HEAD_DIM
B_W = N_HEADS_B * HEAD_DIM
D_IN = QA_W + 2 * KVA_W + 3 * B_W
HEAD_GROUP_W = 256
Q_SCALE = HEAD_DIM ** -0.5

ROW_TILE = 512
NA_ROWS_PER_STEP = 8
MOE_BLK = 512
ROUTER_ROWS = 40
MOE_ROW_STEP = 128
SOFTMAX_STRIP = 32
VMEM_LIMIT = 56 * 1024 * 1024
MOE_VMEM_LIMIT = 60 * 1024 * 1024


def _cparams(sem, vmem=None):
    return pltpu.CompilerParams(dimension_semantics=sem, vmem_limit_bytes=vmem)


def _rms_scale(x):
    return x * lax.rsqrt(jnp.mean(x * x, axis=-1, keepdims=True) + EPS)


def _in_proj_body(x_ref, g_ref, w_ref, o_ref):
    h = (_rms_scale(x_ref[...]) * g_ref[...]).astype(BF16)
    nch = 512
    for j in range(D_IN // nch):
        sl = slice(j * nch, (j + 1) * nch)
        o_ref[:, sl] = jnp.dot(h, w_ref[:, sl], preferred_element_type=F32).astype(BF16)


def _in_proj_kernel(xp_ref, xs_ref, g_ref, w_ref, o_ref, *, n_prompt_blocks):
    i = pl.program_id(0)

    @pl.when(i < n_prompt_blocks)
    def _():
        _in_proj_body(xp_ref, g_ref, w_ref, o_ref)

    @pl.when(i >= n_prompt_blocks)
    def _():
        _in_proj_body(xs_ref, g_ref, w_ref, o_ref)


def _in_proj(xp, xs, g_attn, w_in_bf16):
    tp, ts = xp.shape[0], xs.shape[0]
    npb, nsb = tp // ROW_TILE, ts // ROW_TILE
    return pl.pallas_call(
        functools.partial(_in_proj_kernel, n_prompt_blocks=npb),
        grid=(npb + nsb,),
        in_specs=[
            pl.BlockSpec((ROW_TILE, D_MODEL), lambda i: (jnp.minimum(i, npb - 1), 0)),
            pl.BlockSpec((ROW_TILE, D_MODEL), lambda i: (jnp.maximum(i - npb, 0), 0)),
            pl.BlockSpec((1, D_MODEL), lambda i: (0, 0)),
            pl.BlockSpec((D_MODEL, D_IN), lambda i: (0, 0), pipeline_mode=pl.Buffered(1)),
        ],
        out_specs=pl.BlockSpec((ROW_TILE, D_IN), lambda i: (i, 0)),
        out_shape=jax.ShapeDtypeStruct((tp + ts, D_IN), BF16),
        compiler_params=_cparams(("arbitrary",), VMEM_LIMIT),
        name="in_proj",
    )(xp, xs, g_attn.reshape(1, D_MODEL), w_in_bf16)


def _seq_bounds(pos, bounds):
    lo = jnp.int32(bounds[0])
    hi = jnp.int32(bounds[1])
    for k in range(1, len(bounds) - 1):
        inside = pos >= bounds[k]
        lo = jnp.where(inside, bounds[k], lo)
        hi = jnp.where(inside, bounds[k + 1], hi)
    return lo, hi


def _lane_head(width):
    return lax.broadcasted_iota(I32, (1, width), 1) // HEAD_DIM


def _stack_heads(x, lane_head):
    zero = jnp.zeros_like(x)
    return jnp.concatenate([jnp.where(lane_head == r, x, zero) for r in range(4)], axis=0)


def _diag_heads(o_all, m, lane_head):
    out = jnp.where(lane_head == 0, o_all[0:m], 0.0)
    for r in range(1, 4):
        out = out + jnp.where(lane_head == r, o_all[r * m:(r + 1) * m], 0.0)
    return out


_NT = (((1,), (1,)), ((), ()))


def _win_kernel(sink_ref, q_ref, kp_ref, km_ref, kn_ref, vp_ref, vm_ref, vn_ref, o_ref,
                kbuf, vbuf, bias_scr, s_scr, p_scr, *, bounds):
    i = pl.program_id(0)
    tq = q_ref.shape[0]
    n_sub = tq // WIN
    tok0 = i * tq
    lo, hi = _seq_bounds(tok0, bounds)
    lane_head = _lane_head(HEAD_GROUP_W)

    @pl.when(i == 0)
    def _():
        qi = lax.broadcasted_iota(I32, (WIN, 3 * WIN), 0)
        ki = lax.broadcasted_iota(I32, (WIN, 3 * WIN), 1)
        dist_i = jnp.abs(qi - ki + WIN)
        dist = dist_i.astype(F32)
        for h in range(N_HEADS_A):
            slope = 2.0 ** (-8.0 * (h + 1.0) / N_HEADS_A)
            base = jnp.where(dist_i <= WIN, -(slope * dist), -jnp.inf)
            bias_scr[0, h] = base
            bias_scr[1, h] = jnp.where(ki >= WIN, base, -jnp.inf)
            bias_scr[2, h] = jnp.where(ki < 2 * WIN, base, -jnp.inf)

    er = lax.broadcasted_iota(I32, (HEAD_GROUP_W, HEAD_GROUP_W), 0)
    ec = lax.broadcasted_iota(I32, (HEAD_GROUP_W, HEAD_GROUP_W), 1)
    for g in range(N_KV_A):
        spread = jnp.where((er // HEAD_DIM == g) & (er % HEAD_DIM == ec % HEAD_DIM), 1.0, 0.0).astype(BF16)
        for buf, (p_ref, m_ref, n_ref) in ((kbuf, (kp_ref, km_ref, kn_ref)), (vbuf, (vp_ref, vm_ref, vn_ref))):
            buf[g, 0:WIN, :] = jnp.dot(p_ref[...], spread, preferred_element_type=F32).astype(BF16)
            buf[g, WIN:WIN + tq, :] = jnp.dot(m_ref[...], spread, preferred_element_type=F32).astype(BF16)
            buf[g, WIN + tq:, :] = jnp.dot(n_ref[...], spread, preferred_element_type=F32).astype(BF16)

    def scores(k):
        g, c = divmod(k, n_sub)
        r0 = c * WIN
        qg = q_ref[r0:r0 + WIN, g * HEAD_GROUP_W:(g + 1) * HEAD_GROUP_W] * Q_SCALE
        lhs = _stack_heads(qg, lane_head)
        kwin = kbuf[g, r0:r0 + 3 * WIN, :]
        s_scr[k % 2] = lax.dot_general(lhs, kwin, _NT, preferred_element_type=F32)

    def softmax(k):
        g, c = divmod(k, n_sub)
        buf = k % 2
        variant = 0
        if c == 0:
            variant = jnp.where(tok0 == lo, 1, 0)
        if c == n_sub - 1:
            variant = jnp.where(tok0 + tq == hi, 2, 0)
        for r in range(4):
            h = g * 4 + r
            sink = sink_ref[h]
            for st in range(0, WIN, SOFTMAX_STRIP):
                rows = slice(r * WIN + st, r * WIN + st + SOFTMAX_STRIP)
                t = s_scr[buf, rows, :] + bias_scr[variant, h, st:st + SOFTMAX_STRIP, :]
                m = jnp.maximum(jnp.max(t, axis=-1, keepdims=True), sink)
                p = jnp.exp(t - m)
                den = jnp.sum(p, axis=-1, keepdims=True) + jnp.exp(sink - m)
                p_scr[buf, rows, :] = (p * (1.0 / den)).astype(BF16)

    def output(k):
        g, c = divmod(k, n_sub)
        r0 = c * WIN
        vwin = vbuf[g, r0:r0 + 3 * WIN, :]
        o_all = jnp.dot(p_scr[k % 2], vwin, preferred_element_type=F32)
        o_ref[r0:r0 + WIN, g * HEAD_GROUP_W:(g + 1) * HEAD_GROUP_W] = _diag_heads(o_all, WIN, lane_head)

    n_items = N_KV_A * n_sub
    scores(0)
    for k in range(n_items):
        if k + 1 < n_items:
            scores(k + 1)
        softmax(k)
        output(k)


def _window_attention(proj, sink, bounds):
    t_all = proj.shape[0]
    tq = ROW_TILE
    nblk = t_all // tq
    halo_per_tile = tq // WIN
    n_halo = t_all // WIN
    k_col = (QA_W + 3 * B_W) // KVA_W
    v_col = k_col + 1

    def main_spec(col):
        return pl.BlockSpec((tq, KVA_W), lambda i: (i, col))

    def prev_spec(col):
        return pl.BlockSpec((WIN, KVA_W), lambda i: (jnp.maximum(i * halo_per_tile - 1, 0), col))

    def next_spec(col):
        return pl.BlockSpec((WIN, KVA_W), lambda i: (jnp.minimum((i + 1) * halo_per_tile, n_halo - 1), col))

    return pl.pallas_call(
        functools.partial(_win_kernel, bounds=bounds),
        grid=(nblk,),
        in_specs=[
            pl.BlockSpec(memory_space=pltpu.SMEM),
            pl.BlockSpec((tq, QA_W), lambda i: (i, 0)),
            prev_spec(k_col), main_spec(k_col), next_spec(k_col),
            prev_spec(v_col), main_spec(v_col), next_spec(v_col),
        ],
        out_specs=pl.BlockSpec((tq, QA_W), lambda i: (i, 0)),
        out_shape=jax.ShapeDtypeStruct((t_all, QA_W), F32),
        scratch_shapes=[pltpu.VMEM((N_KV_A, tq + 2 * WIN, KVA_W), BF16), pltpu.VMEM((N_KV_A, tq + 2 * WIN, KVA_W), BF16),
                        pltpu.VMEM((3, N_HEADS_A, WIN, 3 * WIN), F32),
                        pltpu.VMEM((2, 4 * WIN, 3 * WIN), F32), pltpu.VMEM((2, 4 * WIN, 3 * WIN), BF16)],
        compiler_params=_cparams(("arbitrary",), VMEM_LIMIT),
        name="window_attention",
    )(sink, proj, proj, proj, proj, proj, proj, proj)


def _bias_table_kernel(rpb_ref, o_ref):
    g = pl.program_id(0)
    n_rel_rows = 2 * NA_ROWS - 1
    n_rel_cols = 2 * NA_COLS - 1
    qc = lax.broadcasted_iota(I32, (GRID_W, 2 * GRID_W), 0)
    lane = lax.broadcasted_iota(I32, (GRID_W, 2 * GRID_W), 1)
    kc = lane % GRID_W
    second = lane >= GRID_W
    rel = jnp.clip(kc - qc + NA_COLS - 1, 0, n_rel_cols - 1)
    col_start = jnp.clip(qc - NA_COLS // 2, 0, GRID_W - NA_COLS)
    col_valid = (kc >= col_start) & (kc < col_start + NA_COLS)
    for r in range(4):
        h = g * 4 + r
        for j in range(n_rel_rows - 1):
            base_a = (h * n_rel_rows + j) * n_rel_cols
            base_b = base_a + n_rel_cols
            tile = jnp.zeros((GRID_W, 2 * GRID_W), F32)
            for d in range(n_rel_cols):
                val = jnp.where(second, rpb_ref[base_b + d], rpb_ref[base_a + d])
                tile = jnp.where(rel == d, val, tile)
            tile = jnp.where(col_valid, tile, -jnp.inf)
            for j0 in range(NA_ROWS):
                k = j - j0
                if k >= 0 and k % 2 == 0 and k < NA_ROWS:
                    o_ref[j0, r * GRID_W:(r + 1) * GRID_W, k * GRID_W:(k + 2) * GRID_W] = tile


def _bias_table(rpb):
    return pl.pallas_call(
        _bias_table_kernel,
        grid=(4,),
        in_specs=[pl.BlockSpec(memory_space=pltpu.SMEM)],
        out_specs=pl.BlockSpec((NA_ROWS, None, 4 * GRID_W, NA_ROWS * GRID_W), lambda g: (0, g, 0, 0)),
        out_shape=jax.ShapeDtypeStruct((NA_ROWS, 4, 4 * GRID_W, NA_ROWS * GRID_W), F32),
        compiler_params=_cparams(("arbitrary",)),
        name="na_bias_table",
    )(rpb.reshape(-1))


def _na_kernel(q_ref, kp_ref, km_ref, kn_ref, vp_ref, vm_ref, vn_ref, bt_ref, o_ref, kbuf, vbuf, s_scr, p_scr,
               *, row_bounds):
    i = pl.program_id(1)
    rows = NA_ROWS_PER_STEP
    tile = rows * GRID_W
    lane_head = _lane_head(HEAD_GROUP_W)
    for buf, (p_ref, m_ref, n_ref) in ((kbuf, (kp_ref, km_ref, kn_ref)), (vbuf, (vp_ref, vm_ref, vn_ref))):
        buf[0:tile, :] = p_ref[...]
        buf[tile:2 * tile, :] = m_ref[...]
        buf[2 * tile:3 * tile, :] = n_ref[...]
    row0 = i * rows
    lo, hi = _seq_bounds(row0, row_bounds)

    def window(jr):
        r = row0 + jr
        rs = jnp.clip(r - NA_ROWS // 2, lo, hi - NA_ROWS)
        j0 = rs - r + NA_ROWS - 1
        off = pl.multiple_of((rs - row0 + rows) * GRID_W, GRID_W)
        return j0, off

    def scores(jr):
        _, off = window(jr)
        q = q_ref[jr * GRID_W:(jr + 1) * GRID_W, :] * Q_SCALE
        lhs = _stack_heads(q, lane_head)
        kwin = kbuf[pl.ds(off, NA_ROWS * GRID_W), :]
        s_scr[jr % 2] = lax.dot_general(lhs, kwin, _NT, preferred_element_type=F32)

    def softmax(jr):
        j0, _ = window(jr)
        buf = jr % 2
        for st in range(0, 4 * GRID_W, SOFTMAX_STRIP):
            s = s_scr[buf, st:st + SOFTMAX_STRIP, :] + bt_ref[j0, st:st + SOFTMAX_STRIP, :]
            m = jnp.max(s, axis=-1, keepdims=True)
            p = jnp.exp(s - m)
            den = jnp.sum(p, axis=-1, keepdims=True)
            p_scr[buf, st:st + SOFTMAX_STRIP, :] = (p * (1.0 / den)).astype(BF16)

    def output(jr):
        _, off = window(jr)
        vwin = vbuf[pl.ds(off, NA_ROWS * GRID_W), :]
        o_all = jnp.dot(p_scr[jr % 2], vwin, preferred_element_type=F32)
        o_ref[jr * GRID_W:(jr + 1) * GRID_W, :] = _diag_heads(o_all, GRID_W, lane_head)

    scores(0)
    for jr in range(rows):
        if jr + 1 < rows:
            scores(jr + 1)
        softmax(jr)
        output(jr)


def _neighborhood_attention(proj, bias_table, bounds):
    t_all = proj.shape[0]
    tile = NA_ROWS_PER_STEP * GRID_W
    nblk = t_all // tile
    row_bounds = tuple(b // GRID_W for b in bounds)
    q_col = QA_W // HEAD_GROUP_W
    k_col = (QA_W + B_W) // HEAD_GROUP_W
    v_col = (QA_W + 2 * B_W) // HEAD_GROUP_W

    def spec(col, shift):
        return pl.BlockSpec((tile, HEAD_GROUP_W), lambda g, i: (jnp.clip(i + shift, 0, nblk - 1), col + g))

    return pl.pallas_call(
        functools.partial(_na_kernel, row_bounds=row_bounds),
        grid=(4, nblk),
        in_specs=[
            spec(q_col, 0),
            spec(k_col, -1), spec(k_col, 0), spec(k_col, 1),
            spec(v_col, -1), spec(v_col, 0), spec(v_col, 1),
            pl.BlockSpec((NA_ROWS, None, 4 * GRID_W, NA_ROWS * GRID_W), lambda g, i: (0, g, 0, 0)),
        ],
        out_specs=pl.BlockSpec((tile, HEAD_GROUP_W), lambda g, i: (i, g)),
        out_shape=jax.ShapeDtypeStruct((t_all, B_W), F32),
        scratch_shapes=[pltpu.VMEM((3 * tile, HEAD_GROUP_W), BF16), pltpu.VMEM((3 * tile, HEAD_GROUP_W), BF16),
                        pltpu.VMEM((2, 4 * GRID_W, NA_ROWS * GRID_W), F32),
                        pltpu.VMEM((2, 4 * GRID_W, NA_ROWS * GRID_W), BF16)],
        compiler_params=_cparams(("arbitrary", "arbitrary"), VMEM_LIMIT),
        name="neighborhood_attention",
    )(proj, proj, proj, proj, proj, proj, proj, bias_table)


def _split_bf16(x):
    hi = x.astype(BF16)
    lo = (x - hi.astype(F32)).astype(BF16)
    return hi, lo


def _ffn_norm(x1, gffn):
    return _rms_scale(x1) * gffn


def _out_proj_body(x_ref, oa_ref, ob_ref, gmix_ref, wout_ref, gffn_ref, wr_ref, x1_ref, lg_ref):
    ma = (_rms_scale(oa_ref[...]) * gmix_ref[:, :QA_W]).astype(BF16)
    mb = (_rms_scale(ob_ref[...]) * gmix_ref[:, QA_W:]).astype(BF16)
    y = jnp.dot(ma, wout_ref[:QA_W, :], preferred_element_type=F32)
    y = y + jnp.dot(mb, wout_ref[QA_W:, :], preferred_element_type=F32)
    x1 = x_ref[...] + y
    x1_ref[...] = x1
    h2 = _ffn_norm(x1, gffn_ref[...])
    h_hi, h_lo = _split_bf16(h2)
    w_hi, w_lo = _split_bf16(wr_ref[...])
    lg = lax.dot_general(w_hi, h_hi, _NT, preferred_element_type=F32)
    lg = lg + lax.dot_general(w_lo, h_hi, _NT, preferred_element_type=F32)
    lg = lg + lax.dot_general(w_hi, h_lo, _NT, preferred_element_type=F32)
    lg_ref[...] = lg


def _out_proj_kernel(xp_ref, xs_ref, *rest, n_prompt_blocks):
    i = pl.program_id(0)

    @pl.when(i < n_prompt_blocks)
    def _():
        _out_proj_body(xp_ref, *rest)

    @pl.when(i >= n_prompt_blocks)
    def _():
        _out_proj_body(xs_ref, *rest)


def _out_proj(xp, xs, o_a, o_b, g_mix, w_out_bf16, g_ffn, w_router):
    tp, ts = xp.shape[0], xs.shape[0]
    t_all = tp + ts
    npb, nsb = tp // ROW_TILE, ts // ROW_TILE
    row = lambda i: (i, 0)
    const = lambda i: (0, 0)
    return pl.pallas_call(
        functools.partial(_out_proj_kernel, n_prompt_blocks=npb),
        grid=(npb + nsb,),
        in_specs=[
            pl.BlockSpec((ROW_TILE, D_MODEL), lambda i: (jnp.minimum(i, npb - 1), 0)),
            pl.BlockSpec((ROW_TILE, D_MODEL), lambda i: (jnp.maximum(i - npb, 0), 0)),
            pl.BlockSpec((ROW_TILE, QA_W), row),
            pl.BlockSpec((ROW_TILE, B_W), row),
            pl.BlockSpec((1, D_MODEL), const),
            pl.BlockSpec((D_MODEL, D_MODEL), const, pipeline_mode=pl.Buffered(1)),
            pl.BlockSpec((1, D_MODEL), const),
            pl.BlockSpec((ROUTER_ROWS, D_MODEL), const),
        ],
        out_specs=[
            pl.BlockSpec((ROW_TILE, D_MODEL), row),
            pl.BlockSpec((ROUTER_ROWS, ROW_TILE), lambda i: (0, i)),
        ],
        out_shape=[
            jax.ShapeDtypeStruct((t_all, D_MODEL), F32),
            jax.ShapeDtypeStruct((ROUTER_ROWS, t_all), F32),
        ],
        compiler_params=_cparams(("arbitrary",), VMEM_LIMIT),
        name="out_proj_router",
    )(xp, xs, o_a, o_b, g_mix.reshape(1, D_MODEL), w_out_bf16, g_ffn.reshape(1, D_MODEL), w_router)


def _route_kernel(l_ref, dest_ref, gate_ref, blk_ref, pref_scr, *, t_all, n_blk):
    chunk = 256
    lanes = 128
    sub8 = lax.broadcasted_iota(I32, (EPG, t_all), 0)
    gl = jnp.where(sub8 < N_GROUPS, l_ref[N_EXPERTS:N_EXPERTS + EPG, :], -jnp.inf)
    gmax = jnp.max(gl, axis=0, keepdims=True)
    gidx = jnp.min(jnp.where(gl == gmax, sub8, EPG), axis=0, keepdims=True)
    p_grp = 1.0 / jnp.sum(jnp.exp(gl - gmax), axis=0, keepdims=True)

    e_sel = l_ref[(N_GROUPS - 1) * EPG:N_GROUPS * EPG, :]
    for g in range(N_GROUPS - 2, -1, -1):
        e_sel = jnp.where(gidx == g, l_ref[g * EPG:(g + 1) * EPG, :], e_sel)
    ee = jnp.exp(e_sel - jnp.max(e_sel, axis=0, keepdims=True))
    pe = ee * (1.0 / jnp.sum(ee, axis=0, keepdims=True))
    v1 = jnp.max(pe, axis=0, keepdims=True)
    i1 = jnp.min(jnp.where(pe == v1, sub8, EPG), axis=0, keepdims=True)
    pe2 = jnp.where(sub8 == i1, -1.0, pe)
    v2 = jnp.max(pe2, axis=0, keepdims=True)
    i2 = jnp.min(jnp.where(pe2 == v2, sub8, EPG), axis=0, keepdims=True)
    norm = p_grp * (1.0 / (v1 + v2))
    gate_ref[...] = jnp.where(sub8 == 0, v1 * norm, jnp.where(sub8 == 1, v2 * norm, 0.0))
    e1 = gidx * EPG + i1
    e2 = gidx * EPG + i2

    tri = jnp.where(lax.broadcasted_iota(I32, (chunk, chunk), 0) < lax.broadcasted_iota(I32, (chunk, chunk), 1),
                    1.0, 0.0).astype(BF16)
    sub32 = lax.broadcasted_iota(I32, (N_EXPERTS, chunk), 0)
    count = jnp.zeros((N_EXPERTS, 1), F32)
    for c in range(t_all // chunk):
        sl = slice(c * chunk, (c + 1) * chunk)
        member = jnp.where((sub32 == e1[:, sl]) | (sub32 == e2[:, sl]), 1.0, 0.0)
        pref_scr[:, sl] = jnp.dot(member.astype(BF16), tri, preferred_element_type=F32) + count
        count = count + jnp.sum(member, axis=1, keepdims=True)

    nblk_e = jnp.floor((count + (MOE_BLK - 1)) * (1.0 / MOE_BLK))
    nb_l = jnp.broadcast_to(nblk_e, (N_EXPERTS, lanes))
    hi16 = jnp.floor(nb_l * (1.0 / 16.0))
    lo16 = nb_l - 16.0 * hi16
    ltri = jnp.where(lax.broadcasted_iota(I32, (N_EXPERTS, N_EXPERTS), 1) < lax.broadcasted_iota(I32, (N_EXPERTS, N_EXPERTS), 0),
                     1.0, 0.0).astype(BF16)
    first_blk = (16.0 * jnp.dot(ltri, hi16.astype(BF16), preferred_element_type=F32)
                 + jnp.dot(ltri, lo16.astype(BF16), preferred_element_type=F32))
    end_blk = first_blk + nb_l
    pstart = first_blk[:, 0:1] * float(MOE_BLK)

    sub32t = lax.broadcasted_iota(I32, (N_EXPERTS, t_all), 0)
    slot = pref_scr[...] + pstart
    d1 = jnp.sum(jnp.where(sub32t == e1, slot, 0.0), axis=0, keepdims=True)
    d2 = jnp.sum(jnp.where(sub32t == e2, slot, 0.0), axis=0, keepdims=True)
    dest_ref[...] = jnp.where(sub8 == 0, d1, jnp.where(sub8 == 1, d2, 0.0)).astype(I32)

    blk_i = lax.broadcasted_iota(I32, (N_EXPERTS, n_blk), 1).astype(F32)
    owner = jnp.sum(jnp.where(end_blk[:, 0:1] <= blk_i, 1.0, 0.0), axis=0, keepdims=True)
    owner = jnp.minimum(owner, float(N_EXPERTS - 1))
    n_used = jnp.sum(nblk_e, axis=0, keepdims=True)
    inside = (first_blk[:, 0:1] <= blk_i) & (blk_i < end_blk[:, 0:1])
    left = count - (blk_i - first_blk[:, 0:1]) * float(MOE_BLK)
    rows = jnp.sum(jnp.where(inside, jnp.minimum(left, float(MOE_BLK)), 0.0), axis=0, keepdims=True)
    row8 = lax.broadcasted_iota(I32, (8, n_blk), 0)
    blk_ref[...] = jnp.where(row8 == 0, owner, jnp.where(row8 == 1, n_used, jnp.where(row8 == 2, rows, 0.0))).astype(I32)


def _route(logits_t, n_slots):
    t_all = logits_t.shape[1]
    n_blk = pl.cdiv(n_slots // MOE_BLK, 128) * 128
    vmem = pl.BlockSpec(memory_space=pltpu.VMEM)
    return pl.pallas_call(
        functools.partial(_route_kernel, t_all=t_all, n_blk=n_blk),
        in_specs=[vmem],
        out_specs=[vmem, vmem, vmem],
        out_shape=[
            jax.ShapeDtypeStruct((8, t_all), I32),
            jax.ShapeDtypeStruct((8, t_all), F32),
            jax.ShapeDtypeStruct((8, n_blk), I32),
        ],
        scratch_shapes=[pltpu.VMEM((N_EXPERTS, t_all), F32)],
        compiler_params=pltpu.CompilerParams(vmem_limit_bytes=VMEM_LIMIT),
        name="route",
    )(logits_t)


def _slot_map_kernel(d1_ref, d2_ref, init_hbm, slot_ref, sem, *, t_all, t_pad):
    fill = pltpu.make_async_copy(init_hbm, slot_ref, sem)
    fill.start()
    fill.wait()
    unroll = 8

    def body(j, c):
        for u in range(unroll):
            t = j * unroll + u
            slot_ref[d1_ref[t]] = t
            slot_ref[d2_ref[t]] = t_pad + t
        return c

    lax.fori_loop(0, t_all // unroll, body, 0)


def _slot_map(d1, d2, n_slots, t_pad):
    t_all = d1.shape[0]
    init = t_all + jnp.bitwise_and(jnp.arange(n_slots, dtype=I32), 2 * MOE_BLK - 1)
    smem = pl.BlockSpec(memory_space=pltpu.SMEM)
    return pl.pallas_call(
        functools.partial(_slot_map_kernel, t_all=t_all, t_pad=t_pad),
        in_specs=[smem, smem, pl.BlockSpec(memory_space=pl.ANY)],
        out_specs=smem,
        out_shape=jax.ShapeDtypeStruct((n_slots,), I32),
        scratch_shapes=[pltpu.SemaphoreType.DMA(())],
        name="slot_map",
    )(d1, d2, init)


def _expert_changed(b, blk_e_ref):
    prev = blk_e_ref[jnp.maximum(b - 1, 0)]
    return (b == 0) | (blk_e_ref[b] != prev)


def _gather_start(slot_ref, h_hbm, xbuf, sem, blk, buf, r, t_all, t_pad):
    row = slot_ref[blk * MOE_BLK + r]
    tok = jnp.minimum(jnp.where(row >= t_pad, row - t_pad, row), t_all - 1)
    pltpu.make_async_copy(h_hbm.at[pl.ds(tok, 1), :], xbuf.at[buf, pl.ds(r, 1), :], sem.at[buf]).start()


def _gather_wait(h_hbm, xbuf, sem, buf):
    for r in range(MOE_BLK):
        pltpu.make_async_copy(h_hbm.at[pl.ds(0, 1), :], xbuf.at[buf, pl.ds(r, 1), :], sem.at[buf]).wait()


def _row_steps(rows_ref, b):
    return jnp.right_shift(rows_ref[b] + (MOE_ROW_STEP - 1), MOE_ROW_STEP.bit_length() - 1)


def _moe_up_kernel(blk_e_ref, nused_ref, rows_ref, slot_ref, h_hbm, gffn_ref, wg_ref, wu_ref, hid_ref, xbuf, sem,
                   wg_bf, wu_bf, *, t_all, t_pad):
    b = pl.program_id(0)
    nused = nused_ref[0]
    cur = lax.rem(b, 2)
    n_chunks = 4
    rows_per_chunk = MOE_BLK // n_chunks
    nch = D_EXPERT // n_chunks
    steps = _row_steps(rows_ref, b)

    @pl.when(b == 0)
    def _():
        for r in range(MOE_BLK):
            _gather_start(slot_ref, h_hbm, xbuf, sem, 0, 0, r, t_all, t_pad)

    @pl.when(b < nused)
    def _():
        _gather_wait(h_hbm, xbuf, sem, cur)

        @pl.when(_expert_changed(b, blk_e_ref))
        def _():
            wg_bf[...] = wg_ref[...].astype(BF16)
            wu_bf[...] = wu_ref[...].astype(BF16)

    def compute(m):
        nxt = jnp.minimum(b + 1, nused - 1)
        x = _ffn_norm(xbuf[cur, 0:m, :], gffn_ref[...]).astype(BF16)
        for j in range(n_chunks):
            for r in range(j * rows_per_chunk, (j + 1) * rows_per_chunk):
                _gather_start(slot_ref, h_hbm, xbuf, sem, nxt, 1 - cur, r, t_all, t_pad)
            sl = slice(j * nch, (j + 1) * nch)
            gt = jnp.dot(x, wg_bf[:, sl], preferred_element_type=F32)
            up = jnp.dot(x, wu_bf[:, sl], preferred_element_type=F32)
            hid_ref[0:m, sl] = (gt * (1.0 / (1.0 + jnp.exp(-gt))) * up).astype(BF16)
        if m < MOE_BLK:
            hid_ref[m:, :] = jnp.zeros((MOE_BLK - m, D_EXPERT), BF16)

    for k in range(1, MOE_BLK // MOE_ROW_STEP + 1):
        @pl.when((b < nused) & (steps == k))
        def _(k=k):
            compute(k * MOE_ROW_STEP)

    @pl.when(b == nused - 1)
    def _():
        _gather_wait(h_hbm, xbuf, sem, 1 - cur)

    @pl.when(b >= nused)
    def _():
        hid_ref[...] = jnp.zeros_like(hid_ref)


def _scatter_start(slot_ref, ybuf, y_hbm, sem, blk, buf, r):
    row = slot_ref[blk * MOE_BLK + r]
    pltpu.make_async_copy(ybuf.at[buf, pl.ds(r, 1), :], y_hbm.at[pl.ds(row, 1), :], sem.at[buf]).start()


def _scatter_wait(ybuf, y_hbm, sem, buf):
    for r in range(MOE_BLK):
        pltpu.make_async_copy(ybuf.at[buf, pl.ds(r, 1), :], y_hbm.at[pl.ds(0, 1), :], sem.at[buf]).wait()


def _moe_down_kernel(blk_e_ref, nused_ref, rows_ref, slot_ref, hid_ref, wd_ref, y_hbm, ybuf, sem, wd_bf, *, t_all, t_pad):
    b = pl.program_id(0)
    nused = nused_ref[0]
    cur = lax.rem(b, 2)
    n_chunks = 4
    rows_per_chunk = MOE_BLK // n_chunks
    nch = D_MODEL // n_chunks
    steps = _row_steps(rows_ref, b)

    @pl.when(b == 0)
    def _():
        ybuf[...] = jnp.zeros(ybuf.shape, F32)
        for half in range(2):
            for part in range(2):
                row0 = half * t_pad + t_all + part * MOE_BLK
                cp = pltpu.make_async_copy(ybuf.at[0], y_hbm.at[pl.ds(row0, MOE_BLK), :], sem.at[0])
                cp.start()
                cp.wait()

    @pl.when((b >= 2) & (b <= nused))
    def _():
        _scatter_wait(ybuf, y_hbm, sem, cur)

    def compute(m, scatter_previous):
        h = hid_ref[0:m, :]
        for j in range(n_chunks):
            if scatter_previous:
                for r in range(j * rows_per_chunk, (j + 1) * rows_per_chunk):
                    _scatter_start(slot_ref, ybuf, y_hbm, sem, b - 1, 1 - cur, r)
            sl = slice(j * nch, (j + 1) * nch)
            ybuf[cur, 0:m, sl] = jnp.dot(h, wd_bf[:, sl], preferred_element_type=F32)

    @pl.when(b == 0)
    def _():
        wd_bf[...] = wd_ref[...].astype(BF16)
        compute(MOE_BLK, False)

    @pl.when((b >= 1) & (b < nused) & _expert_changed(b, blk_e_ref))
    def _():
        wd_bf[...] = wd_ref[...].astype(BF16)

    for k in range(1, MOE_BLK // MOE_ROW_STEP + 1):
        @pl.when((b >= 1) & (b < nused) & (steps == k))
        def _(k=k):
            compute(k * MOE_ROW_STEP, True)

    @pl.when(b == nused)
    def _():
        for r in range(MOE_BLK):
            _scatter_start(slot_ref, ybuf, y_hbm, sem, b - 1, 1 - cur, r)
        _scatter_wait(ybuf, y_hbm, sem, 1 - cur)


def _blk(b, nused_ref):
    return jnp.minimum(b, nused_ref[0] - 1)


def _moe(x1, g_ffn, slot, blk_e, n_used, blk_rows, w_gate, w_up, w_down, t_pad):
    t_all = x1.shape[0]
    n_slots = slot.shape[0]
    n_blk = n_slots // MOE_BLK
    hid = pl.pallas_call(
        functools.partial(_moe_up_kernel, t_all=t_all, t_pad=t_pad),
        grid_spec=pltpu.PrefetchScalarGridSpec(
            num_scalar_prefetch=4,
            grid=(n_blk,),
            in_specs=[
                pl.BlockSpec(memory_space=pl.ANY),
                pl.BlockSpec((1, D_MODEL), lambda b, be, nu, rw, sl: (0, 0)),
                pl.BlockSpec((None, D_MODEL, D_EXPERT), lambda b, be, nu, rw, sl: (be[_blk(b, nu)], 0, 0)),
                pl.BlockSpec((None, D_MODEL, D_EXPERT), lambda b, be, nu, rw, sl: (be[_blk(b, nu)], 0, 0)),
            ],
            out_specs=pl.BlockSpec((MOE_BLK, D_EXPERT), lambda b, be, nu, rw, sl: (b, 0)),
            scratch_shapes=[pltpu.VMEM((2, MOE_BLK, D_MODEL), F32), pltpu.SemaphoreType.DMA((2,)),
                            pltpu.VMEM((D_MODEL, D_EXPERT), BF16), pltpu.VMEM((D_MODEL, D_EXPERT), BF16)],
        ),
        out_shape=jax.ShapeDtypeStruct((n_slots, D_EXPERT), BF16),
        compiler_params=_cparams(("arbitrary",), MOE_VMEM_LIMIT),
        name="moe_gate_up",
    )(blk_e, n_used, blk_rows, slot, x1, g_ffn.reshape(1, D_MODEL), w_gate, w_up)
    return pl.pallas_call(
        functools.partial(_moe_down_kernel, t_all=t_all, t_pad=t_pad),
        grid_spec=pltpu.PrefetchScalarGridSpec(
            num_scalar_prefetch=4,
            grid=(n_blk,),
            in_specs=[
                pl.BlockSpec((MOE_BLK, D_EXPERT), lambda b, be, nu, rw, sl: (_blk(b, nu), 0)),
                pl.BlockSpec((None, D_EXPERT, D_MODEL), lambda b, be, nu, rw, sl: (be[_blk(b, nu)], 0, 0)),
            ],
            out_specs=pl.BlockSpec(memory_space=pl.ANY),
            scratch_shapes=[pltpu.VMEM((2, MOE_BLK, D_MODEL), F32), pltpu.SemaphoreType.DMA((2,)),
                            pltpu.VMEM((D_EXPERT, D_MODEL), BF16)],
        ),
        out_shape=jax.ShapeDtypeStruct((2 * t_pad, D_MODEL), F32),
        compiler_params=_cparams(("arbitrary",), VMEM_LIMIT),
        name="moe_down",
    )(blk_e, n_used, blk_rows, slot, hid, w_down)


def _combine_body(x1_ref, ya_ref, yb_ref, gate_ref, g_ref, out_ref):
    gt = gate_ref[...]
    x2 = x1_ref[...] + (ya_ref[...] * gt[:, 0:1] + yb_ref[...] * gt[:, 1:2])
    out_ref[...] = _rms_scale(x2) * g_ref[...]


def _combine_kernel(x1_ref, ya_ref, yb_ref, gate_ref, g_ref, outp_ref, outs_ref, *, n_prompt_blocks):
    i = pl.program_id(0)

    @pl.when(i < n_prompt_blocks)
    def _():
        _combine_body(x1_ref, ya_ref, yb_ref, gate_ref, g_ref, outp_ref)

    @pl.when(i >= n_prompt_blocks)
    def _():
        _combine_body(x1_ref, ya_ref, yb_ref, gate_ref, g_ref, outs_ref)


def _combine(x1, gates_t, g_final, y2, tp, ts, t_pad):
    tm = ROW_TILE
    npb, nsb = tp // tm, ts // tm
    second = t_pad // tm
    return pl.pallas_call(
        functools.partial(_combine_kernel, n_prompt_blocks=npb),
        grid=(npb + nsb,),
        in_specs=[
            pl.BlockSpec((tm, D_MODEL), lambda i: (i, 0)),
            pl.BlockSpec((tm, D_MODEL), lambda i: (i, 0)),
            pl.BlockSpec((tm, D_MODEL), lambda i: (second + i, 0)),
            pl.BlockSpec((tm, 2), lambda i: (i, 0)),
            pl.BlockSpec((1, D_MODEL), lambda i: (0, 0)),
        ],
        out_specs=[
            pl.BlockSpec((tm, D_MODEL), lambda i: (jnp.minimum(i, npb - 1), 0)),
            pl.BlockSpec((tm, D_MODEL), lambda i: (jnp.maximum(i - npb, 0), 0)),
        ],
        out_shape=[jax.ShapeDtypeStruct((tp, D_MODEL), F32), jax.ShapeDtypeStruct((ts, D_MODEL), F32)],
        compiler_params=_cparams(("arbitrary",), VMEM_LIMIT),
        name="combine_final_norm",
    )(x1, y2, y2, gates_t, g_final.reshape(1, D_MODEL))


def _layer(xp, xs, g_attn, w_in, g_mix, sink_a, rpb_b, w_out, g_ffn, w_rg, w_re, w_gate, w_up, w_down, g_final):
    tp, ts = xp.shape[0], xs.shape[0]
    t_all = tp + ts
    bounds = (0, tp, t_all)
    s0, s1, s2, s3, s4 = QA_W, QA_W + KVA_W, QA_W + 2 * KVA_W, QA_W + 2 * KVA_W + B_W, QA_W + 2 * KVA_W + 2 * B_W
    w_in_p = jnp.concatenate([w_in[:, :s0], w_in[:, s2:s3], w_in[:, s3:s4], w_in[:, s4:], w_in[:, s0:s1], w_in[:, s1:s2]],
                             axis=1).astype(BF16)
    w_router = jnp.concatenate([w_re.T, w_rg.T, jnp.zeros((ROUTER_ROWS - N_EXPERTS - N_GROUPS, D_MODEL), F32)], axis=0)

    proj = _in_proj(xp, xs, g_attn, w_in_p)
    o_a = _window_attention(proj, sink_a, bounds)
    o_b = _neighborhood_attention(proj, _bias_table(rpb_b), bounds)
    x1, logits_t = _out_proj(xp, xs, o_a, o_b, g_mix, w_out.astype(BF16), g_ffn, w_router)

    n_slots = 2 * t_all + N_EXPERTS * MOE_BLK
    n_blk = n_slots // MOE_BLK
    t_pad = t_all + 2 * MOE_BLK
    dest, gate, blk = _route(logits_t, n_slots)
    slot = _slot_map(dest[0], dest[1], n_slots, t_pad)
    y2 = _moe(x1, g_ffn, slot, blk[0, :n_blk], blk[1, :1], blk[2, :n_blk], w_gate, w_up, w_down, t_pad)
    return _combine(x1, gate[:2].T, g_final, y2, tp, ts, t_pad)


def kernel(x_prompt, x_sample, g_attn, w_in, g_mix, sink_a, rpb_b, w_out, g_ffn, w_router_group, w_router_expert,
           w_gate, w_up, w_down, g_final):
    assert x_prompt.shape[0] == 1 and x_sample.shape[0] == 1 and g_attn.shape[0] == 1
    yp, ys = _layer(x_prompt[0], x_sample[0], g_attn[0], w_in[0], g_mix[0], sink_a[0], rpb_b[0], w_out[0], g_ffn[0],
                    w_router_group[0], w_router_expert[0], w_gate[0], w_up[0], w_down[0], g_final)
    return yp[None], ys[None]
```

```python
import functools

import jax
import jax.numpy as jnp
from jax import lax
from jax.experimental import pallas as pl
from jax.experimental.pallas import tpu as pltpu

F32 = jnp.float32
BF16 = jnp.bfloat16
I32 = jnp.int32

D_MODEL = 2048
HEAD_DIM = 64
N_HEADS_A = 16
N_KV_A = 4
N_HEADS_B = 16
WIN = 128
GRID_W = 64
NA_ROWS = 8
NA_COLS = 16
N_GROUPS = 4
EPG = 8
N_EXPERTS = 32
D_EXPERT = 1024
EPS = 1e-6

QA_W = N_HEADS_A * HEAD_DIM
KVA_W = N_KV_A * HEAD_DIM
B_W = N_HEADS_B * HEAD_DIM
D_IN = QA_W + 2 * KVA_W + 3 * B_W
HEAD_GROUP_W = 256
Q_SCALE = HEAD_DIM ** -0.5

ROW_TILE = 512
NA_ROWS_PER_STEP = 8
MOE_BLK = 512
ROUTER_ROWS = 40
MOE_ROW_STEP = 128
SOFTMAX_STRIP = 32
VMEM_LIMIT = 56 * 1024 * 1024
MOE_VMEM_LIMIT = 60 * 1024 * 1024


def _cparams(sem, vmem=None):
    return pltpu.CompilerParams(dimension_semantics=sem, vmem_limit_bytes=vmem)


def _rms_scale(x):
    return x * lax.rsqrt(jnp.mean(x * x, axis=-1, keepdims=True) + EPS)


def _in_proj_body(x_ref, g_ref, w_ref, o_ref):
    h = (_rms_scale(x_ref[...]) * g_ref[...]).astype(BF16)
    nch = 512
    for j in range(D_IN // nch):
        sl = slice(j * nch, (j + 1) * nch)
        o_ref[:, sl] = jnp.dot(h, w_ref[:, sl], preferred_element_type=F32).astype(BF16)


def _in_proj_kernel(xp_ref, xs_ref, g_ref, w_ref, o_ref, *, n_prompt_blocks):
    i = pl.program_id(0)

    @pl.when(i < n_prompt_blocks)
    def _():
        _in_proj_body(xp_ref, g_ref, w_ref, o_ref)

    @pl.when(i >= n_prompt_blocks)
    def _():
        _in_proj_body(xs_ref, g_ref, w_ref, o_ref)


def _in_proj(xp, xs, g_attn, w_in_bf16):
    tp, ts = xp.shape[0], xs.shape[0]
    npb, nsb = tp // ROW_TILE, ts // ROW_TILE
    return pl.pallas_call(
        functools.partial(_in_proj_kernel, n_prompt_blocks=npb),
        grid=(npb + nsb,),
        in_specs=[
            pl.BlockSpec((ROW_TILE, D_MODEL), lambda i: (jnp.minimum(i, npb - 1), 0)),
            pl.BlockSpec((ROW_TILE, D_MODEL), lambda i: (jnp.maximum(i - npb, 0), 0)),
            pl.BlockSpec((1, D_MODEL), lambda i: (0, 0)),
            pl.BlockSpec((D_MODEL, D_IN), lambda i: (0, 0), pipeline_mode=pl.Buffered(1)),
        ],
        out_specs=pl.BlockSpec((ROW_TILE, D_IN), lambda i: (i, 0)),
        out_shape=jax.ShapeDtypeStruct((tp + ts, D_IN), BF16),
        compiler_params=_cparams(("arbitrary",), VMEM_LIMIT),
        name="in_proj",
    )(xp, xs, g_attn.reshape(1, D_MODEL), w_in_bf16)


def _seq_bounds(pos, bounds):
    lo = jnp.int32(bounds[0])
    hi = jnp.int32(bounds[1])
    for k in range(1, len(bounds) - 1):
        inside = pos >= bounds[k]
        lo = jnp.where(inside, bounds[k], lo)
        hi = jnp.where(inside, bounds[k + 1], hi)
    return lo, hi


def _lane_head(width):
    return lax.broadcasted_iota(I32, (1, width), 1) // HEAD_DIM


def _stack_heads(x, lane_head):
    zero = jnp.zeros_like(x)
    return jnp.concatenate([jnp.where(lane_head == r, x, zero) for r in range(4)], axis=0)


def _diag_heads(o_all, m, lane_head):
    out = jnp.where(lane_head == 0, o_all[0:m], 0.0)
    for r in range(1, 4):
        out = out + jnp.where(lane_head == r, o_all[r * m:(r + 1) * m], 0.0)
    return out


_NT = (((1,), (1,)), ((), ()))


def _win_kernel(sink_ref, q_ref, kp_ref, km_ref, kn_ref, vp_ref, vm_ref, vn_ref, o_ref,
                kbuf, vbuf, bias_scr, s_scr, p_scr, *, bounds):
    i = pl.program_id(0)
    tq = q_ref.shape[0]
    n_sub = tq // WIN
    tok0 = i * tq
    lo, hi = _seq_bounds(tok0, bounds)
    lane_head = _lane_head(HEAD_GROUP_W)

    @pl.when(i == 0)
    def _():
        qi = lax.broadcasted_iota(I32, (WIN, 3 * WIN), 0)
        ki = lax.broadcasted_iota(I32, (WIN, 3 * WIN), 1)
        dist_i = jnp.abs(qi - ki + WIN)
        dist = dist_i.astype(F32)
        for h in range(N_HEADS_A):
            slope = 2.0 ** (-8.0 * (h + 1.0) / N_HEADS_A)
            base = jnp.where(dist_i <= WIN, -(slope * dist), -jnp.inf)
            bias_scr[0, h] = base
            bias_scr[1, h] = jnp.where(ki >= WIN, base, -jnp.inf)
            bias_scr[2, h] = jnp.where(ki < 2 * WIN, base, -jnp.inf)

    er = lax.broadcasted_iota(I32, (HEAD_GROUP_W, HEAD_GROUP_W), 0)
    ec = lax.broadcasted_iota(I32, (HEAD_GROUP_W, HEAD_GROUP_W), 1)
    for g in range(N_KV_A):
        spread = jnp.where((er // HEAD_DIM == g) & (er % HEAD_DIM == ec % HEAD_DIM), 1.0, 0.0).astype(BF16)
        for buf, (p_ref, m_ref, n_ref) in ((kbuf, (kp_ref, km_ref, kn_ref)), (vbuf, (vp_ref, vm_ref, vn_ref))):
            buf[g, 0:WIN, :] = jnp.dot(p_ref[...], spread, preferred_element_type=F32).astype(BF16)
            buf[g, WIN:WIN + tq, :] = jnp.dot(m_ref[...], spread, preferred_element_type=F32).astype(BF16)
            buf[g, WIN + tq:, :] = jnp.dot(n_ref[...], spread, preferred_element_type=F32).astype(BF16)

    def scores(k):
        g, c = divmod(k, n_sub)
        r0 = c * WIN
        qg = q_ref[r0:r0 + WIN, g * HEAD_GROUP_W:(g + 1) * HEAD_GROUP_W] * Q_SCALE
        lhs = _stack_heads(qg, lane_head)
        kwin = kbuf[g, r0:r0 + 3 * WIN, :]
        s_scr[k % 2] = lax.dot_general(lhs, kwin, _NT, preferred_element_type=F32)

    def softmax(k):
        g, c = divmod(k, n_sub)
        buf = k % 2
        variant = 0
        if c == 0:
            variant = jnp.where(tok0 == lo, 1, 0)
        if c == n_sub - 1:
            variant = jnp.where(tok0 + tq == hi, 2, 0)
        for r in range(4):
            h = g * 4 + r
            sink = sink_ref[h]
            for st in range(0, WIN, SOFTMAX_STRIP):
                rows = slice(r * WIN + st, r * WIN + st + SOFTMAX_STRIP)
                t = s_scr[buf, rows, :] + bias_scr[variant, h, st:st + SOFTMAX_STRIP, :]
                m = jnp.maximum(jnp.max(t, axis=-1, keepdims=True), sink)
                p = jnp.exp(t - m)
                den = jnp.sum(p, axis=-1, keepdims=True) + jnp.exp(sink - m)
                p_scr[buf, rows, :] = (p * (1.0 / den)).astype(BF16)

    def output(k):
        g, c = divmod(k, n_sub)
        r0 = c * WIN
        vwin = vbuf[g, r0:r0 + 3 * WIN, :]
        o_all = jnp.dot(p_scr[k % 2], vwin, preferred_element_type=F32)
        o_ref[r0:r0 + WIN, g * HEAD_GROUP_W:(g + 1) * HEAD_GROUP_W] = _diag_heads(o_all, WIN, lane_head)

    n_items = N_KV_A * n_sub
    scores(0)
    for k in range(n_items):
        if k + 1 < n_items:
            scores(k + 1)
        softmax(k)
        output(k)


def _window_attention(proj, sink, bounds):
    t_all = proj.shape[0]
    tq = ROW_TILE
    nblk = t_all // tq
    halo_per_tile = tq // WIN
    n_halo = t_all // WIN
    k_col = (QA_W + 3 * B_W) // KVA_W
    v_col = k_col + 1

    def main_spec(col):
        return pl.BlockSpec((tq, KVA_W), lambda i: (i, col))

    def prev_spec(col):
        return pl.BlockSpec((WIN, KVA_W), lambda i: (jnp.maximum(i * halo_per_tile - 1, 0), col))

    def next_spec(col):
        return pl.BlockSpec((WIN, KVA_W), lambda i: (jnp.minimum((i + 1) * halo_per_tile, n_halo - 1), col))

    return pl.pallas_call(
        functools.partial(_win_kernel, bounds=bounds),
        grid=(nblk,),
        in_specs=[
            pl.BlockSpec(memory_space=pltpu.SMEM),
            pl.BlockSpec((tq, QA_W), lambda i: (i, 0)),
            prev_spec(k_col), main_spec(k_col), next_spec(k_col),
            prev_spec(v_col), main_spec(v_col), next_spec(v_col),
        ],
        out_specs=pl.BlockSpec((tq, QA_W), lambda i: (i, 0)),
        out_shape=jax.ShapeDtypeStruct((t_all, QA_W), F32),
        scratch_shapes=[pltpu.VMEM((N_KV_A, tq + 2 * WIN, KVA_W), BF16), pltpu.VMEM((N_KV_A, tq + 2 * WIN, KVA_W), BF16),
                        pltpu.VMEM((3, N_HEADS_A, WIN, 3 * WIN), F32),
                        pltpu.VMEM((2, 4 * WIN, 3 * WIN), F32), pltpu.VMEM((2, 4 * WIN, 3 * WIN), BF16)],
        compiler_params=_cparams(("arbitrary",), VMEM_LIMIT),
        name="window_attention",
    )(sink, proj, proj, proj, proj, proj, proj, proj)


def _bias_table_kernel(rpb_ref, o_ref):
    g = pl.program_id(0)
    n_rel_rows = 2 * NA_ROWS - 1
    n_rel_cols = 2 * NA_COLS - 1
    qc = lax.broadcasted_iota(I32, (GRID_W, 2 * GRID_W), 0)
    lane = lax.broadcasted_iota(I32, (GRID_W, 2 * GRID_W), 1)
    kc = lane % GRID_W
    second = lane >= GRID_W
    rel = jnp.clip(kc - qc + NA_COLS - 1, 0, n_rel_cols - 1)
    col_start = jnp.clip(qc - NA_COLS // 2, 0, GRID_W - NA_COLS)
    col_valid = (kc >= col_start) & (kc < col_start + NA_COLS)
    for r in range(4):
        h = g * 4 + r
        for j in range(n_rel_rows - 1):
            base_a = (h * n_rel_rows + j) * n_rel_cols
            base_b = base_a + n_rel_cols
            tile = jnp.zeros((GRID_W, 2 * GRID_W), F32)
            for d in range(n_rel_cols):
                val = jnp.where(second, rpb_ref[base_b + d], rpb_ref[base_a + d])
                tile = jnp.where(rel == d, val, tile)
            tile = jnp.where(col_valid, tile, -jnp.inf)
            for j0 in range(NA_ROWS):
                k = j - j0
                if k >= 0 and k % 2 == 0 and k < NA_ROWS:
                    o_ref[j0, r * GRID_W:(r + 1) * GRID_W, k * GRID_W:(k + 2) * GRID_W] = tile


def _bias_table(rpb):
    return pl.pallas_call(
        _bias_table_kernel,
        grid=(4,),
        in_specs=[pl.BlockSpec(memory_space=pltpu.SMEM)],
        out_specs=pl.BlockSpec((NA_ROWS, None, 4 * GRID_W, NA_ROWS * GRID_W), lambda g: (0, g, 0, 0)),
        out_shape=jax.ShapeDtypeStruct((NA_ROWS, 4, 4 * GRID_W, NA_ROWS * GRID_W), F32),
        compiler_params=_cparams(("arbitrary",)),
        name="na_bias_table",
    )(rpb.reshape(-1))


def _na_kernel(q_ref, kp_ref, km_ref, kn_ref, vp_ref, vm_ref, vn_ref, bt_ref, o_ref, kbuf, vbuf, s_scr, p_scr,
               *, row_bounds):
    i = pl.program_id(1)
    rows = NA_ROWS_PER_STEP
    tile = rows * GRID_W
    lane_head = _lane_head(HEAD_GROUP_W)
    for buf, (p_ref, m_ref, n_ref) in ((kbuf, (kp_ref, km_ref, kn_ref)), (vbuf, (vp_ref, vm_ref, vn_ref))):
        buf[0:tile, :] = p_ref[...]
        buf[tile:2 * tile, :] = m_ref[...]
        buf[2 * tile:3 * tile, :] = n_ref[...]
    row0 = i * rows
    lo, hi = _seq_bounds(row0, row_bounds)

    def window(jr):
        r = row0 + jr
        rs = jnp.clip(r - NA_ROWS // 2, lo, hi - NA_ROWS)
        j0 = rs - r + NA_ROWS - 1
        off = pl.multiple_of((rs - row0 + rows) * GRID_W, GRID_W)
        return j0, off

    def scores(jr):
        _, off = window(jr)
        q = q_ref[jr * GRID_W:(jr + 1) * GRID_W, :] * Q_SCALE
        lhs = _stack_heads(q, lane_head)
        kwin = kbuf[pl.ds(off, NA_ROWS * GRID_W), :]
        s_scr[jr % 2] = lax.dot_general(lhs, kwin, _NT, preferred_element_type=F32)

    def softmax(jr):
        j0, _ = window(jr)
        buf = jr % 2
        for st in range(0, 4 * GRID_W, SOFTMAX_STRIP):
            s = s_scr[buf, st:st + SOFTMAX_STRIP, :] + bt_ref[j0, st:st + SOFTMAX_STRIP, :]
            m = jnp.max(s, axis=-1, keepdims=True)
            p = jnp.exp(s - m)
            den = jnp.sum(p, axis=-1, keepdims=True)
            p_scr[buf, st:st + SOFTMAX_STRIP, :] = (p * (1.0 / den)).astype(BF16)

    def output(jr):
        _, off = window(jr)
        vwin = vbuf[pl.ds(off, NA_ROWS * GRID_W), :]
        o_all = jnp.dot(p_scr[jr % 2], vwin, preferred_element_type=F32)
        o_ref[jr * GRID_W:(jr + 1) * GRID_W, :] = _diag_heads(o_all, GRID_W, lane_head)

    scores(0)
    for jr in range(rows):
        if jr + 1 < rows:
            scores(jr + 1)
        softmax(jr)
        output(jr)


def _neighborhood_attention(proj, bias_table, bounds):
    t_all = proj.shape[0]
    tile = NA_ROWS_PER_STEP * GRID_W
    nblk = t_all // tile
    row_bounds = tuple(b // GRID_W for b in bounds)
    q_col = QA_W // HEAD_GROUP_W
    k_col = (QA_W + B_W) // HEAD_GROUP_W
    v_col = (QA_W + 2 * B_W) // HEAD_GROUP_W

    def spec(col, shift):
        return pl.BlockSpec((tile, HEAD_GROUP_W), lambda g, i: (jnp.clip(i + shift, 0, nblk - 1), col + g))

    return pl.pallas_call(
        functools.partial(_na_kernel, row_bounds=row_bounds),
        grid=(4, nblk),
        in_specs=[
            spec(q_col, 0),
            spec(k_col, -1), spec(k_col, 0), spec(k_col, 1),
            spec(v_col, -1), spec(v_col, 0), spec(v_col, 1),
            pl.BlockSpec((NA_ROWS, None, 4 * GRID_W, NA_ROWS * GRID_W), lambda g, i: (0, g, 0, 0)),
        ],
        out_specs=pl.BlockSpec((tile, HEAD_GROUP_W), lambda g, i: (i, g)),
        out_shape=jax.ShapeDtypeStruct((t_all, B_W), F32),
        scratch_shapes=[pltpu.VMEM((3 * tile, HEAD_GROUP_W), BF16), pltpu.VMEM((3 * tile, HEAD_GROUP_W), BF16),
                        pltpu.VMEM((2, 4 * GRID_W, NA_ROWS * GRID_W), F32),
                        pltpu.VMEM((2, 4 * GRID_W, NA_ROWS * GRID_W), BF16)],
        compiler_params=_cparams(("arbitrary", "arbitrary"), VMEM_LIMIT),
        name="neighborhood_attention",
    )(proj, proj, proj, proj, proj, proj, proj, bias_table)


def _ffn_norm(x1, gffn):
    return _rms_scale(x1) * gffn


def _out_proj_body(x_ref, oa_ref, ob_ref, gmix_ref, wout_ref, gffn_ref, wr_ref, x1_ref, lg_ref):
    ma = (_rms_scale(oa_ref[...]) * gmix_ref[:, :QA_W]).astype(BF16)
    mb = (_rms_scale(ob_ref[...]) * gmix_ref[:, QA_W:]).astype(BF16)
    y = jnp.dot(ma, wout_ref[:QA_W, :], preferred_element_type=F32)
    y = y + jnp.dot(mb, wout_ref[QA_W:, :], preferred_element_type=F32)
    x1 = x_ref[...] + y
    x1_ref[...] = x1
    h2 = _ffn_norm(x1, gffn_ref[...])
    lg_ref[...] = lax.dot_general(wr_ref[...].astype(BF16), h2.astype(BF16), _NT, preferred_element_type=F32)


def _out_proj_kernel(xp_ref, xs_ref, *rest, n_prompt_blocks):
    i = pl.program_id(0)

    @pl.when(i < n_prompt_blocks)
    def _():
        _out_proj_body(xp_ref, *rest)

    @pl.when(i >= n_prompt_blocks)
    def _():
        _out_proj_body(xs_ref, *rest)


def _out_proj(xp, xs, o_a, o_b, g_mix, w_out_bf16, g_ffn, w_router):
    tp, ts = xp.shape[0], xs.shape[0]
    t_all = tp + ts
    npb, nsb = tp // ROW_TILE, ts // ROW_TILE
    row = lambda i: (i, 0)
    const = lambda i: (0, 0)
    return pl.pallas_call(
        functools.partial(_out_proj_kernel, n_prompt_blocks=npb),
        grid=(npb + nsb,),
        in_specs=[
            pl.BlockSpec((ROW_TILE, D_MODEL), lambda i: (jnp.minimum(i, npb - 1), 0)),
            pl.BlockSpec((ROW_TILE, D_MODEL), lambda i: (jnp.maximum(i - npb, 0), 0)),
            pl.BlockSpec((ROW_TILE, QA_W), row),
            pl.BlockSpec((ROW_TILE, B_W), row),
            pl.BlockSpec((1, D_MODEL), const),
            pl.BlockSpec((D_MODEL, D_MODEL), const, pipeline_mode=pl.Buffered(1)),
            pl.BlockSpec((1, D_MODEL), const),
            pl.BlockSpec((ROUTER_ROWS, D_MODEL), const),
        ],
        out_specs=[
            pl.BlockSpec((ROW_TILE, D_MODEL), row),
            pl.BlockSpec((ROUTER_ROWS, ROW_TILE), lambda i: (0, i)),
        ],
        out_shape=[
            jax.ShapeDtypeStruct((t_all, D_MODEL), F32),
            jax.ShapeDtypeStruct((ROUTER_ROWS, t_all), F32),
        ],
        compiler_params=_cparams(("arbitrary",), VMEM_LIMIT),
        name="out_proj_router",
    )(xp, xs, o_a, o_b, g_mix.reshape(1, D_MODEL), w_out_bf16, g_ffn.reshape(1, D_MODEL), w_router)


def _route_kernel(l_ref, dest_ref, gate_ref, blk_ref, pref_scr, *, t_all, n_blk):
    chunk = 256
    lanes = 128
    sub8 = lax.broadcasted_iota(I32, (EPG, t_all), 0)
    gl = jnp.where(sub8 < N_GROUPS, l_ref[N_EXPERTS:N_EXPERTS + EPG, :], -jnp.inf)
    gmax = jnp.max(gl, axis=0, keepdims=True)
    gidx = jnp.min(jnp.where(gl == gmax, sub8, EPG), axis=0, keepdims=True)
    p_grp = 1.0 / jnp.sum(jnp.exp(gl - gmax), axis=0, keepdims=True)

    e_sel = l_ref[(N_GROUPS - 1) * EPG:N_GROUPS * EPG, :]
    for g in range(N_GROUPS - 2, -1, -1):
        e_sel = jnp.where(gidx == g, l_ref[g * EPG:(g + 1) * EPG, :], e_sel)
    ee = jnp.exp(e_sel - jnp.max(e_sel, axis=0, keepdims=True))
    pe = ee * (1.0 / jnp.sum(ee, axis=0, keepdims=True))
    v1 = jnp.max(pe, axis=0, keepdims=True)
    i1 = jnp.min(jnp.where(pe == v1, sub8, EPG), axis=0, keepdims=True)
    pe2 = jnp.where(sub8 == i1, -1.0, pe)
    v2 = jnp.max(pe2, axis=0, keepdims=True)
    i2 = jnp.min(jnp.where(pe2 == v2, sub8, EPG), axis=0, keepdims=True)
    norm = p_grp * (1.0 / (v1 + v2))
    gate_ref[...] = jnp.where(sub8 == 0, v1 * norm, jnp.where(sub8 == 1, v2 * norm, 0.0))
    e1 = gidx * EPG + i1
    e2 = gidx * EPG + i2

    tri = jnp.where(lax.broadcasted_iota(I32, (chunk, chunk), 0) < lax.broadcasted_iota(I32, (chunk, chunk), 1),
                    1.0, 0.0).astype(BF16)
    sub32 = lax.broadcasted_iota(I32, (N_EXPERTS, chunk), 0)
    count = jnp.zeros((N_EXPERTS, 1), F32)
    for c in range(t_all // chunk):
        sl = slice(c * chunk, (c + 1) * chunk)
        member = jnp.where((sub32 == e1[:, sl]) | (sub32 == e2[:, sl]), 1.0, 0.0)
        pref_scr[:, sl] = jnp.dot(member.astype(BF16), tri, preferred_element_type=F32) + count
        count = count + jnp.sum(member, axis=1, keepdims=True)

    nblk_e = jnp.floor((count + (MOE_BLK - 1)) * (1.0 / MOE_BLK))
    nb_l = jnp.broadcast_to(nblk_e, (N_EXPERTS, lanes))
    hi16 = jnp.floor(nb_l * (1.0 / 16.0))
    lo16 = nb_l - 16.0 * hi16
    ltri = jnp.where(lax.broadcasted_iota(I32, (N_EXPERTS, N_EXPERTS), 1) < lax.broadcasted_iota(I32, (N_EXPERTS, N_EXPERTS), 0),
                     1.0, 0.0).astype(BF16)
    first_blk = (16.0 * jnp.dot(ltri, hi16.astype(BF16), preferred_element_type=F32)
                 + jnp.dot(ltri, lo16.astype(BF16), preferred_element_type=F32))
    end_blk = first_blk + nb_l
    pstart = first_blk[:, 0:1] * float(MOE_BLK)

    sub32t = lax.broadcasted_iota(I32, (N_EXPERTS, t_all), 0)
    slot = pref_scr[...] + pstart
    d1 = jnp.sum(jnp.where(sub32t == e1, slot, 0.0), axis=0, keepdims=True)
    d2 = jnp.sum(jnp.where(sub32t == e2, slot, 0.0), axis=0, keepdims=True)
    dest_ref[...] = jnp.where(sub8 == 0, d1, jnp.where(sub8 == 1, d2, 0.0)).astype(I32)

    blk_i = lax.broadcasted_iota(I32, (N_EXPERTS, n_blk), 1).astype(F32)
    owner = jnp.sum(jnp.where(end_blk[:, 0:1] <= blk_i, 1.0, 0.0), axis=0, keepdims=True)
    owner = jnp.minimum(owner, float(N_EXPERTS - 1))
    n_used = jnp.sum(nblk_e, axis=0, keepdims=True)
    inside = (first_blk[:, 0:1] <= blk_i) & (blk_i < end_blk[:, 0:1])
    left = count - (blk_i - first_blk[:, 0:1]) * float(MOE_BLK)
    rows = jnp.sum(jnp.where(inside, jnp.minimum(left, float(MOE_BLK)), 0.0), axis=0, keepdims=True)
    row8 = lax.broadcasted_iota(I32, (8, n_blk), 0)
    blk_ref[...] = jnp.where(row8 == 0, owner, jnp.where(row8 == 1, n_used, jnp.where(row8 == 2, rows, 0.0))).astype(I32)


def _route(logits_t, n_slots):
    t_all = logits_t.shape[1]
    n_blk = pl.cdiv(n_slots // MOE_BLK, 128) * 128
    vmem = pl.BlockSpec(memory_space=pltpu.VMEM)
    return pl.pallas_call(
        functools.partial(_route_kernel, t_all=t_all, n_blk=n_blk),
        in_specs=[vmem],
        out_specs=[vmem, vmem, vmem],
        out_shape=[
            jax.ShapeDtypeStruct((8, t_all), I32),
            jax.ShapeDtypeStruct((8, t_all), F32),
            jax.ShapeDtypeStruct((8, n_blk), I32),
        ],
        scratch_shapes=[pltpu.VMEM((N_EXPERTS, t_all), F32)],
        compiler_params=pltpu.CompilerParams(vmem_limit_bytes=VMEM_LIMIT),
        name="route",
    )(logits_t)


def _slot_map_kernel(d1_ref, d2_ref, init_hbm, slot_ref, sem, *, t_all, t_pad):
    fill = pltpu.make_async_copy(init_hbm, slot_ref, sem)
    fill.start()
    fill.wait()
    unroll = 8

    def body(j, c):
        for u in range(unroll):
            t = j * unroll + u
            slot_ref[d1_ref[t]] = t
            slot_ref[d2_ref[t]] = t_pad + t
        return c

    lax.fori_loop(0, t_all // unroll, body, 0)


def _slot_map(d1, d2, n_slots, t_pad):
    t_all = d1.shape[0]
    init = t_all + jnp.bitwise_and(jnp.arange(n_slots, dtype=I32), 2 * MOE_BLK - 1)
    smem = pl.BlockSpec(memory_space=pltpu.SMEM)
    return pl.pallas_call(
        functools.partial(_slot_map_kernel, t_all=t_all, t_pad=t_pad),
        in_specs=[smem, smem, pl.BlockSpec(memory_space=pl.ANY)],
        out_specs=smem,
        out_shape=jax.ShapeDtypeStruct((n_slots,), I32),
        scratch_shapes=[pltpu.SemaphoreType.DMA(())],
        name="slot_map",
    )(d1, d2, init)


def _expert_changed(b, blk_e_ref):
    prev = blk_e_ref[jnp.maximum(b - 1, 0)]
    return (b == 0) | (blk_e_ref[b] != prev)


def _gather_start(tok_ref, h_hbm, xbuf, sem, blk, buf, r):
    tok = tok_ref[blk * MOE_BLK + r]
    pltpu.make_async_copy(h_hbm.at[pl.ds(tok, 1), :], xbuf.at[buf, pl.ds(r, 1), :], sem.at[buf]).start()


def _gather_wait(h_hbm, xbuf, sem, buf):
    for r in range(MOE_BLK):
        pltpu.make_async_copy(h_hbm.at[pl.ds(0, 1), :], xbuf.at[buf, pl.ds(r, 1), :], sem.at[buf]).wait()


def _row_steps(rows_ref, b):
    return jnp.right_shift(rows_ref[b] + (MOE_ROW_STEP - 1), MOE_ROW_STEP.bit_length() - 1)


def _moe_up_kernel(blk_e_ref, nused_ref, rows_ref, tok_ref, h_hbm, gffn_ref, wg_ref, wu_ref, hid_ref, xbuf, sem,
                   wg_bf, wu_bf):
    b = pl.program_id(0)
    nused = nused_ref[0]
    cur = lax.rem(b, 2)
    n_chunks = 4
    rows_per_chunk = MOE_BLK // n_chunks
    nch = D_EXPERT // n_chunks
    steps = _row_steps(rows_ref, b)

    @pl.when(b == 0)
    def _():
        for r in range(MOE_BLK):
            _gather_start(tok_ref, h_hbm, xbuf, sem, 0, 0, r)

    @pl.when(b < nused)
    def _():
        _gather_wait(h_hbm, xbuf, sem, cur)

        @pl.when(_expert_changed(b, blk_e_ref))
        def _():
            wg_bf[...] = wg_ref[...].astype(BF16)
            wu_bf[...] = wu_ref[...].astype(BF16)

    def compute(m):
        nxt = jnp.minimum(b + 1, nused - 1)
        x = _ffn_norm(xbuf[cur, 0:m, :], gffn_ref[...]).astype(BF16)
        for j in range(n_chunks):
            for r in range(j * rows_per_chunk, (j + 1) * rows_per_chunk):
                _gather_start(tok_ref, h_hbm, xbuf, sem, nxt, 1 - cur, r)
            sl = slice(j * nch, (j + 1) * nch)
            gt = jnp.dot(x, wg_bf[:, sl], preferred_element_type=F32)
            up = jnp.dot(x, wu_bf[:, sl], preferred_element_type=F32)
            hid_ref[0:m, sl] = (gt * (1.0 / (1.0 + jnp.exp(-gt))) * up).astype(BF16)
        if m < MOE_BLK:
            hid_ref[m:, :] = jnp.zeros((MOE_BLK - m, D_EXPERT), BF16)

    for k in range(1, MOE_BLK // MOE_ROW_STEP + 1):
        @pl.when((b < nused) & (steps == k))
        def _(k=k):
            compute(k * MOE_ROW_STEP)

    @pl.when(b == nused - 1)
    def _():
        _gather_wait(h_hbm, xbuf, sem, 1 - cur)

    @pl.when(b >= nused)
    def _():
        hid_ref[...] = jnp.zeros_like(hid_ref)


def _scatter_start(slot_ref, ybuf, y_hbm, sem, blk, buf, r):
    row = slot_ref[blk * MOE_BLK + r]
    pltpu.make_async_copy(ybuf.at[buf, pl.ds(r, 1), :], y_hbm.at[pl.ds(row, 1), :], sem.at[buf]).start()


def _scatter_wait(ybuf, y_hbm, sem, buf):
    for r in range(MOE_BLK):
        pltpu.make_async_copy(ybuf.at[buf, pl.ds(r, 1), :], y_hbm.at[pl.ds(0, 1), :], sem.at[buf]).wait()


def _moe_down_kernel(blk_e_ref, nused_ref, rows_ref, slot_ref, hid_ref, wd_ref, y_hbm, ybuf, sem, wd_bf, *, t_all, t_pad):
    b = pl.program_id(0)
    nused = nused_ref[0]
    cur = lax.rem(b, 2)
    n_chunks = 4
    rows_per_chunk = MOE_BLK // n_chunks
    nch = D_MODEL // n_chunks
    steps = _row_steps(rows_ref, b)

    @pl.when(b == 0)
    def _():
        ybuf[...] = jnp.zeros(ybuf.shape, F32)
        for half in range(2):
            for part in range(2):
                row0 = half * t_pad + t_all + part * MOE_BLK
                cp = pltpu.make_async_copy(ybuf.at[0], y_hbm.at[pl.ds(row0, MOE_BLK), :], sem.at[0])
                cp.start()
                cp.wait()

    @pl.when((b >= 2) & (b <= nused))
    def _():
        _scatter_wait(ybuf, y_hbm, sem, cur)

    def compute(m, scatter_previous):
        h = hid_ref[0:m, :]
        for j in range(n_chunks):
            if scatter_previous:
                for r in range(j * rows_per_chunk, (j + 1) * rows_per_chunk):
                    _scatter_start(slot_ref, ybuf, y_hbm, sem, b - 1, 1 - cur, r)
            sl = slice(j * nch, (j + 1) * nch)
            ybuf[cur, 0:m, sl] = jnp.dot(h, wd_bf[:, sl], preferred_element_type=F32)

    @pl.when(b == 0)
    def _():
        wd_bf[...] = wd_ref[...].astype(BF16)
        compute(MOE_BLK, False)

    @pl.when((b >= 1) & (b < nused) & _expert_changed(b, blk_e_ref))
    def _():
        wd_bf[...] = wd_ref[...].astype(BF16)

    for k in range(1, MOE_BLK // MOE_ROW_STEP + 1):
        @pl.when((b >= 1) & (b < nused) & (steps == k))
        def _(k=k):
            compute(k * MOE_ROW_STEP, True)

    @pl.when(b == nused)
    def _():
        for r in range(MOE_BLK):
            _scatter_start(slot_ref, ybuf, y_hbm, sem, b - 1, 1 - cur, r)
        _scatter_wait(ybuf, y_hbm, sem, 1 - cur)


def _blk(b, nused_ref):
    return jnp.minimum(b, nused_ref[0] - 1)


def _moe(x1, g_ffn, slot, blk_e, n_used, blk_rows, w_gate, w_up, w_down, t_pad):
    t_all = x1.shape[0]
    n_slots = slot.shape[0]
    n_blk = n_slots // MOE_BLK
    slot_tok = jnp.minimum(jnp.where(slot >= t_pad, slot - t_pad, slot), t_all - 1)
    hid = pl.pallas_call(
        _moe_up_kernel,
        grid_spec=pltpu.PrefetchScalarGridSpec(
            num_scalar_prefetch=4,
            grid=(n_blk,),
            in_specs=[
                pl.BlockSpec(memory_space=pl.ANY),
                pl.BlockSpec((1, D_MODEL), lambda b, be, nu, rw, sl: (0, 0)),
                pl.BlockSpec((None, D_MODEL, D_EXPERT), lambda b, be, nu, rw, sl: (be[_blk(b, nu)], 0, 0)),
                pl.BlockSpec((None, D_MODEL, D_EXPERT), lambda b, be, nu, rw, sl: (be[_blk(b, nu)], 0, 0)),
            ],
            out_specs=pl.BlockSpec((MOE_BLK, D_EXPERT), lambda b, be, nu, rw, sl: (b, 0)),
            scratch_shapes=[pltpu.VMEM((2, MOE_BLK, D_MODEL), F32), pltpu.SemaphoreType.DMA((2,)),
                            pltpu.VMEM((D_MODEL, D_EXPERT), BF16), pltpu.VMEM((D_MODEL, D_EXPERT), BF16)],
        ),
        out_shape=jax.ShapeDtypeStruct((n_slots, D_EXPERT), BF16),
        compiler_params=_cparams(("arbitrary",), MOE_VMEM_LIMIT),
        name="moe_gate_up",
    )(blk_e, n_used, blk_rows, slot_tok, x1, g_ffn.reshape(1, D_MODEL), w_gate, w_up)
    return pl.pallas_call(
        functools.partial(_moe_down_kernel, t_all=t_all, t_pad=t_pad),
        grid_spec=pltpu.PrefetchScalarGridSpec(
            num_scalar_prefetch=4,
            grid=(n_blk,),
            in_specs=[
                pl.BlockSpec((MOE_BLK, D_EXPERT), lambda b, be, nu, rw, sl: (_blk(b, nu), 0)),
                pl.BlockSpec((None, D_EXPERT, D_MODEL), lambda b, be, nu, rw, sl: (be[_blk(b, nu)], 0, 0)),
            ],
            out_specs=pl.BlockSpec(memory_space=pl.ANY),
            scratch_shapes=[pltpu.VMEM((2, MOE_BLK, D_MODEL), F32), pltpu.SemaphoreType.DMA((2,)),
                            pltpu.VMEM((D_EXPERT, D_MODEL), BF16)],
        ),
        out_shape=jax.ShapeDtypeStruct((2 * t_pad, D_MODEL), F32),
        compiler_params=_cparams(("arbitrary",), VMEM_LIMIT),
        name="moe_down",
    )(blk_e, n_used, blk_rows, slot, hid, w_down)


def _combine_body(x1_ref, ya_ref, yb_ref, gate_ref, g_ref, out_ref):
    gt = gate_ref[...]
    x2 = x1_ref[...] + (ya_ref[...] * gt[:, 0:1] + yb_ref[...] * gt[:, 1:2])
    out_ref[...] = _rms_scale(x2) * g_ref[...]


def _combine_kernel(x1_ref, ya_ref, yb_ref, gate_ref, g_ref, outp_ref, outs_ref, *, n_prompt_blocks):
    i = pl.program_id(0)

    @pl.when(i < n_prompt_blocks)
    def _():
        _combine_body(x1_ref, ya_ref, yb_ref, gate_ref, g_ref, outp_ref)

    @pl.when(i >= n_prompt_blocks)
    def _():
        _combine_body(x1_ref, ya_ref, yb_ref, gate_ref, g_ref, outs_ref)


def _combine(x1, gates_t, g_final, y2, tp, ts, t_pad):
    tm = ROW_TILE
    npb, nsb = tp // tm, ts // tm
    second = t_pad // tm
    return pl.pallas_call(
        functools.partial(_combine_kernel, n_prompt_blocks=npb),
        grid=(npb + nsb,),
        in_specs=[
            pl.BlockSpec((tm, D_MODEL), lambda i: (i, 0)),
            pl.BlockSpec((tm, D_MODEL), lambda i: (i, 0)),
            pl.BlockSpec((tm, D_MODEL), lambda i: (second + i, 0)),
            pl.BlockSpec((tm, 2), lambda i: (i, 0)),
            pl.BlockSpec((1, D_MODEL), lambda i: (0, 0)),
        ],
        out_specs=[
            pl.BlockSpec((tm, D_MODEL), lambda i: (jnp.minimum(i, npb - 1), 0)),
            pl.BlockSpec((tm, D_MODEL), lambda i: (jnp.maximum(i - npb, 0), 0)),
        ],
        out_shape=[jax.ShapeDtypeStruct((tp, D_MODEL), F32), jax.ShapeDtypeStruct((ts, D_MODEL), F32)],
        compiler_params=_cparams(("arbitrary",), VMEM_LIMIT),
        name="combine_final_norm",
    )(x1, y2, y2, gates_t, g_final.reshape(1, D_MODEL))


def _layer(xp, xs, g_attn, w_in, g_mix, sink_a, rpb_b, w_out, g_ffn, w_rg, w_re, w_gate, w_up, w_down, g_final):
    tp, ts = xp.shape[0], xs.shape[0]
    t_all = tp + ts
    bounds = (0, tp, t_all)
    s0, s1, s2, s3, s4 = QA_W, QA_W + KVA_W, QA_W + 2 * KVA_W, QA_W + 2 * KVA_W + B_W, QA_W + 2 * KVA_W + 2 * B_W
    w_in_p = jnp.concatenate([w_in[:, :s0], w_in[:, s2:s3], w_in[:, s3:s4], w_in[:, s4:], w_in[:, s0:s1], w_in[:, s1:s2]],
                             axis=1).astype(BF16)
    w_router = jnp.concatenate([w_re.T, w_rg.T, jnp.zeros((ROUTER_ROWS - N_EXPERTS - N_GROUPS, D_MODEL), F32)], axis=0)

    proj = _in_proj(xp, xs, g_attn, w_in_p)
    o_a = _window_attention(proj, sink_a, bounds)
    o_b = _neighborhood_attention(proj, _bias_table(rpb_b), bounds)
    x1, logits_t = _out_proj(xp, xs, o_a, o_b, g_mix, w_out.astype(BF16), g_ffn, w_router)

    n_slots = 2 * t_all + N_EXPERTS * MOE_BLK
    n_blk = n_slots // MOE_BLK
    t_pad = t_all + 2 * MOE_BLK
    dest, gate, blk = _route(logits_t, n_slots)
    slot = _slot_map(dest[0], dest[1], n_slots, t_pad)
    y2 = _moe(x1, g_ffn, slot, blk[0, :n_blk], blk[1, :1], blk[2, :n_blk], w_gate, w_up, w_down, t_pad)
    return _combine(x1, gate[:2].T, g_final, y2, tp, ts, t_pad)


def kernel(x_prompt, x_sample, g_attn, w_in, g_mix, sink_a, rpb_b, w_out, g_ffn, w_router_group, w_router_expert,
           w_gate, w_up, w_down, g_final):
    assert x_prompt.shape[0] == 1 and x_sample.shape[0] == 1 and g_attn.shape[0] == 1
    yp, ys = _layer(x_prompt[0], x_sample[0], g_attn[0], w_in[0], g_mix[0], sink_a[0], rpb_b[0], w_out[0], g_ffn[0],
                    w_router_group[0], w_router_expert[0], w_gate[0], w_up[0], w_down[0], g_final)
    return yp[None], ys[None]
```

```python
import functools

import jax
import jax.numpy as jnp
from jax import lax
from jax.experimental import pallas as pl
from jax.experimental.pallas import tpu as pltpu

F32 = jnp.float32
BF16 = jnp.bfloat16
I32 = jnp.int32

D_MODEL = 2048
HEAD_DIM = 64
N_HEADS_A = 16
N_KV_A = 4
N_HEADS_B = 16
WIN = 128
GRID_W = 64
NA_ROWS = 8
NA_COLS = 16
N_GROUPS = 4
EPG = 8
N_EXPERTS = 32
D_EXPERT = 1024
EPS = 1e-6

QA_W = N_HEADS_A * HEAD_DIM
KVA_W = N_KV_A * HEAD_DIM
B_W = N_HEADS_B * HEAD_DIM
D_IN = QA_W + 2 * KVA_W + 3 * B_W
HEAD_GROUP_W = 256
Q_SCALE = HEAD_DIM ** -0.5

ROW_TILE = 512
NA_ROWS_PER_STEP = 8
MOE_BLK = 512
ROUTER_ROWS = 40
MOE_ROW_STEP = 128
SOFTMAX_STRIP = 32
VMEM_LIMIT = 56 * 1024 * 1024
MOE_VMEM_LIMIT = 60 * 1024 * 1024


def _cparams(sem, vmem=None):
    return pltpu.CompilerParams(dimension_semantics=sem, vmem_limit_bytes=vmem)


def _rms_scale(x):
    return x * lax.rsqrt(jnp.mean(x * x, axis=-1, keepdims=True) + EPS)


def _in_proj_body(x_ref, g_ref, w_ref, o_ref):
    h = (_rms_scale(x_ref[...]) * g_ref[...]).astype(BF16)
    nch = 512
    for j in range(D_IN // nch):
        sl = slice(j * nch, (j + 1) * nch)
        o_ref[:, sl] = jnp.dot(h, w_ref[:, sl], preferred_element_type=F32).astype(BF16)


def _in_proj_kernel(xp_ref, xs_ref, g_ref, w_ref, o_ref, *, n_prompt_blocks):
    i = pl.program_id(0)

    @pl.when(i < n_prompt_blocks)
    def _():
        _in_proj_body(xp_ref, g_ref, w_ref, o_ref)

    @pl.when(i >= n_prompt_blocks)
    def _():
        _in_proj_body(xs_ref, g_ref, w_ref, o_ref)


def _in_proj(xp, xs, g_attn, w_in_bf16):
    tp, ts = xp.shape[0], xs.shape[0]
    npb, nsb = tp // ROW_TILE, ts // ROW_TILE
    return pl.pallas_call(
        functools.partial(_in_proj_kernel, n_prompt_blocks=npb),
        grid=(npb + nsb,),
        in_specs=[
            pl.BlockSpec((ROW_TILE, D_MODEL), lambda i: (jnp.minimum(i, npb - 1), 0)),
            pl.BlockSpec((ROW_TILE, D_MODEL), lambda i: (jnp.maximum(i - npb, 0), 0)),
            pl.BlockSpec((1, D_MODEL), lambda i: (0, 0)),
            pl.BlockSpec((D_MODEL, D_IN), lambda i: (0, 0), pipeline_mode=pl.Buffered(1)),
        ],
        out_specs=pl.BlockSpec((ROW_TILE, D_IN), lambda i: (i, 0)),
        out_shape=jax.ShapeDtypeStruct((tp + ts, D_IN), BF16),
        compiler_params=_cparams(("arbitrary",), VMEM_LIMIT),
        name="in_proj",
    )(xp, xs, g_attn.reshape(1, D_MODEL), w_in_bf16)


def _seq_bounds(pos, bounds):
    lo = jnp.int32(bounds[0])
    hi = jnp.int32(bounds[1])
    for k in range(1, len(bounds) - 1):
        inside = pos >= bounds[k]
        lo = jnp.where(inside, bounds[k], lo)
        hi = jnp.where(inside, bounds[k + 1], hi)
    return lo, hi


def _lane_head(width):
    return lax.broadcasted_iota(I32, (1, width), 1) // HEAD_DIM


def _stack_heads(x, lane_head):
    zero = jnp.zeros_like(x)
    return jnp.concatenate([jnp.where(lane_head == r, x, zero) for r in range(4)], axis=0)


def _diag_heads(o_all, m, lane_head):
    out = jnp.where(lane_head == 0, o_all[0:m], 0.0)
    for r in range(1, 4):
        out = out + jnp.where(lane_head == r, o_all[r * m:(r + 1) * m], 0.0)
    return out


_NT = (((1,), (1,)), ((), ()))


def _win_kernel(sink_ref, q_ref, kp_ref, km_ref, kn_ref, vp_ref, vm_ref, vn_ref, o_ref,
                kbuf, vbuf, bias_scr, s_scr, p_scr, *, bounds):
    i = pl.program_id(0)
    tq = q_ref.shape[0]
    n_sub = tq // WIN
    tok0 = i * tq
    lo, hi = _seq_bounds(tok0, bounds)
    lane_head = _lane_head(HEAD_GROUP_W)

    @pl.when(i == 0)
    def _():
        qi = lax.broadcasted_iota(I32, (WIN, 3 * WIN), 0)
        ki = lax.broadcasted_iota(I32, (WIN, 3 * WIN), 1)
        dist_i = jnp.abs(qi - ki + WIN)
        dist = dist_i.astype(F32)
        for h in range(N_HEADS_A):
            slope = 2.0 ** (-8.0 * (h + 1.0) / N_HEADS_A)
            base = jnp.where(dist_i <= WIN, -(slope * dist), -jnp.inf)
            bias_scr[0, h] = base
            bias_scr[1, h] = jnp.where(ki >= WIN, base, -jnp.inf)
            bias_scr[2, h] = jnp.where(ki < 2 * WIN, base, -jnp.inf)

    er = lax.broadcasted_iota(I32, (HEAD_GROUP_W, HEAD_GROUP_W), 0)
    ec = lax.broadcasted_iota(I32, (HEAD_GROUP_W, HEAD_GROUP_W), 1)
    for g in range(N_KV_A):
        spread = jnp.where((er // HEAD_DIM == g) & (er % HEAD_DIM == ec % HEAD_DIM), 1.0, 0.0).astype(BF16)
        for buf, (p_ref, m_ref, n_ref) in ((kbuf, (kp_ref, km_ref, kn_ref)), (vbuf, (vp_ref, vm_ref, vn_ref))):
            buf[g, 0:WIN, :] = jnp.dot(p_ref[...], spread, preferred_element_type=F32).astype(BF16)
            buf[g, WIN:WIN + tq, :] = jnp.dot(m_ref[...], spread, preferred_element_type=F32).astype(BF16)
            buf[g, WIN + tq:, :] = jnp.dot(n_ref[...], spread, preferred_element_type=F32).astype(BF16)

    def scores(k):
        g, c = divmod(k, n_sub)
        r0 = c * WIN
        qg = q_ref[r0:r0 + WIN, g * HEAD_GROUP_W:(g + 1) * HEAD_GROUP_W] * Q_SCALE
        lhs = _stack_heads(qg, lane_head)
        kwin = kbuf[g, r0:r0 + 3 * WIN, :]
        s_scr[k % 2] = lax.dot_general(lhs, kwin, _NT, preferred_element_type=F32)

    def softmax(k):
        g, c = divmod(k, n_sub)
        buf = k % 2
        variant = 0
        if c == 0:
            variant = jnp.where(tok0 == lo, 1, 0)
        if c == n_sub - 1:
            variant = jnp.where(tok0 + tq == hi, 2, 0)
        for r in range(4):
            h = g * 4 + r
            sink = sink_ref[h]
            for st in range(0, WIN, SOFTMAX_STRIP):
                rows = slice(r * WIN + st, r * WIN + st + SOFTMAX_STRIP)
                t = s_scr[buf, rows, :] + bias_scr[variant, h, st:st + SOFTMAX_STRIP, :]
                m = jnp.maximum(jnp.max(t, axis=-1, keepdims=True), sink)
                p = jnp.exp(t - m)
                den = jnp.sum(p, axis=-1, keepdims=True) + jnp.exp(sink - m)
                p_scr[buf, rows, :] = (p * (1.0 / den)).astype(BF16)

    def output(k):
        g, c = divmod(k, n_sub)
        r0 = c * WIN
        vwin = vbuf[g, r0:r0 + 3 * WIN, :]
        o_all = jnp.dot(p_scr[k % 2], vwin, preferred_element_type=F32)
        o_ref[r0:r0 + WIN, g * HEAD_GROUP_W:(g + 1) * HEAD_GROUP_W] = _diag_heads(o_all, WIN, lane_head)

    n_items = N_KV_A * n_sub
    scores(0)
    for k in range(n_items):
        if k + 1 < n_items:
            scores(k + 1)
        softmax(k)
        output(k)


def _window_attention(proj, sink, bounds):
    t_all = proj.shape[0]
    tq = ROW_TILE
    nblk = t_all // tq
    halo_per_tile = tq // WIN
    n_halo = t_all // WIN
    k_col = (QA_W + 3 * B_W) // KVA_W
    v_col = k_col + 1

    def main_spec(col):
        return pl.BlockSpec((tq, KVA_W), lambda i: (i, col))

    def prev_spec(col):
        return pl.BlockSpec((WIN, KVA_W), lambda i: (jnp.maximum(i * halo_per_tile - 1, 0), col))

    def next_spec(col):
        return pl.BlockSpec((WIN, KVA_W), lambda i: (jnp.minimum((i + 1) * halo_per_tile, n_halo - 1), col))

    return pl.pallas_call(
        functools.partial(_win_kernel, bounds=bounds),
        grid=(nblk,),
        in_specs=[
            pl.BlockSpec(memory_space=pltpu.SMEM),
            pl.BlockSpec((tq, QA_W), lambda i: (i, 0)),
            prev_spec(k_col), main_spec(k_col), next_spec(k_col),
            prev_spec(v_col), main_spec(v_col), next_spec(v_col),
        ],
        out_specs=pl.BlockSpec((tq, QA_W), lambda i: (i, 0)),
        out_shape=jax.ShapeDtypeStruct((t_all, QA_W), F32),
        scratch_shapes=[pltpu.VMEM((N_KV_A, tq + 2 * WIN, KVA_W), BF16), pltpu.VMEM((N_KV_A, tq + 2 * WIN, KVA_W), BF16),
                        pltpu.VMEM((3, N_HEADS_A, WIN, 3 * WIN), F32),
                        pltpu.VMEM((2, 4 * WIN, 3 * WIN), F32), pltpu.VMEM((2, 4 * WIN, 3 * WIN), BF16)],
        compiler_params=_cparams(("arbitrary",), VMEM_LIMIT),
        name="window_attention",
    )(sink, proj, proj, proj, proj, proj, proj, proj)


def _bias_table_kernel(rpb_ref, o_ref):
    g = pl.program_id(0)
    n_rel_rows = 2 * NA_ROWS - 1
    n_rel_cols = 2 * NA_COLS - 1
    qc = lax.broadcasted_iota(I32, (GRID_W, 2 * GRID_W), 0)
    lane = lax.broadcasted_iota(I32, (GRID_W, 2 * GRID_W), 1)
    kc = lane % GRID_W
    second = lane >= GRID_W
    rel = jnp.clip(kc - qc + NA_COLS - 1, 0, n_rel_cols - 1)
    col_start = jnp.clip(qc - NA_COLS // 2, 0, GRID_W - NA_COLS)
    col_valid = (kc >= col_start) & (kc < col_start + NA_COLS)
    for r in range(4):
        h = g * 4 + r
        for j in range(n_rel_rows - 1):
            base_a = (h * n_rel_rows + j) * n_rel_cols
            base_b = base_a + n_rel_cols
            tile = jnp.zeros((GRID_W, 2 * GRID_W), F32)
            for d in range(n_rel_cols):
                val = jnp.where(second, rpb_ref[base_b + d], rpb_ref[base_a + d])
                tile = jnp.where(rel == d, val, tile)
            tile = jnp.where(col_valid, tile, -jnp.inf)
            for j0 in range(NA_ROWS):
                k = j - j0
                if k >= 0 and k % 2 == 0 and k < NA_ROWS:
                    o_ref[j0, r * GRID_W:(r + 1) * GRID_W, k * GRID_W:(k + 2) * GRID_W] = tile


def _bias_table(rpb):
    return pl.pallas_call(
        _bias_table_kernel,
        grid=(4,),
        in_specs=[pl.BlockSpec(memory_space=pltpu.SMEM)],
        out_specs=pl.BlockSpec((NA_ROWS, None, 4 * GRID_W, NA_ROWS * GRID_W), lambda g: (0, g, 0, 0)),
        out_shape=jax.ShapeDtypeStruct((NA_ROWS, 4, 4 * GRID_W, NA_ROWS * GRID_W), F32),
        compiler_params=_cparams(("arbitrary",)),
        name="na_bias_table",
    )(rpb.reshape(-1))


def _na_kernel(q_ref, kp_ref, km_ref, kn_ref, vp_ref, vm_ref, vn_ref, bt_ref, o_ref, kbuf, vbuf, s_scr, p_scr,
               *, row_bounds):
    i = pl.program_id(1)
    rows = NA_ROWS_PER_STEP
    tile = rows * GRID_W
    lane_head = _lane_head(HEAD_GROUP_W)
    for buf, (p_ref, m_ref, n_ref) in ((kbuf, (kp_ref, km_ref, kn_ref)), (vbuf, (vp_ref, vm_ref, vn_ref))):
        buf[0:tile, :] = p_ref[...]
        buf[tile:2 * tile, :] = m_ref[...]
        buf[2 * tile:3 * tile, :] = n_ref[...]
    row0 = i * rows
    lo, hi = _seq_bounds(row0, row_bounds)

    def window(jr):
        r = row0 + jr
        rs = jnp.clip(r - NA_ROWS // 2, lo, hi - NA_ROWS)
        j0 = rs - r + NA_ROWS - 1
        off = pl.multiple_of((rs - row0 + rows) * GRID_W, GRID_W)
        return j0, off

    def scores(jr):
        _, off = window(jr)
        q = q_ref[jr * GRID_W:(jr + 1) * GRID_W, :] * Q_SCALE
        lhs = _stack_heads(q, lane_head)
        kwin = kbuf[pl.ds(off, NA_ROWS * GRID_W), :]
        s_scr[jr % 2] = lax.dot_general(lhs, kwin, _NT, preferred_element_type=F32)

    def softmax(jr):
        j0, _ = window(jr)
        buf = jr % 2
        for st in range(0, 4 * GRID_W, SOFTMAX_STRIP):
            s = s_scr[buf, st:st + SOFTMAX_STRIP, :] + bt_ref[j0, st:st + SOFTMAX_STRIP, :]
            m = jnp.max(s, axis=-1, keepdims=True)
            p = jnp.exp(s - m)
            den = jnp.sum(p, axis=-1, keepdims=True)
            p_scr[buf, st:st + SOFTMAX_STRIP, :] = (p * (1.0 / den)).astype(BF16)

    def output(jr):
        _, off = window(jr)
        vwin = vbuf[pl.ds(off, NA_ROWS * GRID_W), :]
        o_all = jnp.dot(p_scr[jr % 2], vwin, preferred_element_type=F32)
        o_ref[jr * GRID_W:(jr + 1) * GRID_W, :] = _diag_heads(o_all, GRID_W, lane_head)

    scores(0)
    for jr in range(rows):
        if jr + 1 < rows:
            scores(jr + 1)
        softmax(jr)
        output(jr)


def _neighborhood_attention(proj, bias_table, bounds):
    t_all = proj.shape[0]
    tile = NA_ROWS_PER_STEP * GRID_W
    nblk = t_all // tile
    row_bounds = tuple(b // GRID_W for b in bounds)
    q_col = QA_W // HEAD_GROUP_W
    k_col = (QA_W + B_W) // HEAD_GROUP_W
    v_col = (QA_W + 2 * B_W) // HEAD_GROUP_W

    def spec(col, shift):
        return pl.BlockSpec((tile, HEAD_GROUP_W), lambda g, i: (jnp.clip(i + shift, 0, nblk - 1), col + g))

    return pl.pallas_call(
        functools.partial(_na_kernel, row_bounds=row_bounds),
        grid=(4, nblk),
        in_specs=[
            spec(q_col, 0),
            spec(k_col, -1), spec(k_col, 0), spec(k_col, 1),
            spec(v_col, -1), spec(v_col, 0), spec(v_col, 1),
            pl.BlockSpec((NA_ROWS, None, 4 * GRID_W, NA_ROWS * GRID_W), lambda g, i: (0, g, 0, 0)),
        ],
        out_specs=pl.BlockSpec((tile, HEAD_GROUP_W), lambda g, i: (i, g)),
        out_shape=jax.ShapeDtypeStruct((t_all, B_W), F32),
        scratch_shapes=[pltpu.VMEM((3 * tile, HEAD_GROUP_W), BF16), pltpu.VMEM((3 * tile, HEAD_GROUP_W), BF16),
                        pltpu.VMEM((2, 4 * GRID_W, NA_ROWS * GRID_W), F32),
                        pltpu.VMEM((2, 4 * GRID_W, NA_ROWS * GRID_W), BF16)],
        compiler_params=_cparams(("arbitrary", "arbitrary"), VMEM_LIMIT),
        name="neighborhood_attention",
    )(proj, proj, proj, proj, proj, proj, proj, bias_table)


def _ffn_norm(x1, gffn):
    return _rms_scale(x1) * gffn


def _out_proj_body(x_ref, oa_ref, ob_ref, gmix_ref, wout_ref, gffn_ref, wr_ref, x1_ref, lg_ref):
    ma = (_rms_scale(oa_ref[...]) * gmix_ref[:, :QA_W]).astype(BF16)
    mb = (_rms_scale(ob_ref[...]) * gmix_ref[:, QA_W:]).astype(BF16)
    y = jnp.dot(ma, wout_ref[:QA_W, :], preferred_element_type=F32)
    y = y + jnp.dot(mb, wout_ref[QA_W:, :], preferred_element_type=F32)
    x1 = x_ref[...] + y
    x1_ref[...] = x1
    h2 = _ffn_norm(x1, gffn_ref[...])
    lg_ref[...] = lax.dot_general(wr_ref[...].astype(BF16), h2.astype(BF16), _NT, preferred_element_type=F32)


def _out_proj_kernel(xp_ref, xs_ref, *rest, n_prompt_blocks):
    i = pl.program_id(0)

    @pl.when(i < n_prompt_blocks)
    def _():
        _out_proj_body(xp_ref, *rest)

    @pl.when(i >= n_prompt_blocks)
    def _():
        _out_proj_body(xs_ref, *rest)


def _out_proj(xp, xs, o_a, o_b, g_mix, w_out_bf16, g_ffn, w_router):
    tp, ts = xp.shape[0], xs.shape[0]
    t_all = tp + ts
    npb, nsb = tp // ROW_TILE, ts // ROW_TILE
    row = lambda i: (i, 0)
    const = lambda i: (0, 0)
    return pl.pallas_call(
        functools.partial(_out_proj_kernel, n_prompt_blocks=npb),
        grid=(npb + nsb,),
        in_specs=[
            pl.BlockSpec((ROW_TILE, D_MODEL), lambda i: (jnp.minimum(i, npb - 1), 0)),
            pl.BlockSpec((ROW_TILE, D_MODEL), lambda i: (jnp.maximum(i - npb, 0), 0)),
            pl.BlockSpec((ROW_TILE, QA_W), row),
            pl.BlockSpec((ROW_TILE, B_W), row),
            pl.BlockSpec((1, D_MODEL), const),
            pl.BlockSpec((D_MODEL, D_MODEL), const, pipeline_mode=pl.Buffered(1)),
            pl.BlockSpec((1, D_MODEL), const),
            pl.BlockSpec((ROUTER_ROWS, D_MODEL), const),
        ],
        out_specs=[
            pl.BlockSpec((ROW_TILE, D_MODEL), row),
            pl.BlockSpec((ROUTER_ROWS, ROW_TILE), lambda i: (0, i)),
        ],
        out_shape=[
            jax.ShapeDtypeStruct((t_all, D_MODEL), F32),
            jax.ShapeDtypeStruct((ROUTER_ROWS, t_all), F32),
        ],
        compiler_params=_cparams(("arbitrary",), VMEM_LIMIT),
        name="out_proj_router",
    )(xp, xs, o_a, o_b, g_mix.reshape(1, D_MODEL), w_out_bf16, g_ffn.reshape(1, D_MODEL), w_router)


def _route_kernel(l_ref, dest_ref, gate_ref, blk_ref, pref_scr, *, t_all, n_blk):
    chunk = 256
    lanes = 128
    sub8 = lax.broadcasted_iota(I32, (EPG, t_all), 0)
    gl = jnp.where(sub8 < N_GROUPS, l_ref[N_EXPERTS:N_EXPERTS + EPG, :], -jnp.inf)
    gmax = jnp.max(gl, axis=0, keepdims=True)
    gidx = jnp.min(jnp.where(gl == gmax, sub8, EPG), axis=0, keepdims=True)
    p_grp = 1.0 / jnp.sum(jnp.exp(gl - gmax), axis=0, keepdims=True)

    e_sel = l_ref[(N_GROUPS - 1) * EPG:N_GROUPS * EPG, :]
    for g in range(N_GROUPS - 2, -1, -1):
        e_sel = jnp.where(gidx == g, l_ref[g * EPG:(g + 1) * EPG, :], e_sel)
    ee = jnp.exp(e_sel - jnp.max(e_sel, axis=0, keepdims=True))
    pe = ee * (1.0 / jnp.sum(ee, axis=0, keepdims=True))
    v1 = jnp.max(pe, axis=0, keepdims=True)
    i1 = jnp.min(jnp.where(pe == v1, sub8, EPG), axis=0, keepdims=True)
    pe2 = jnp.where(sub8 == i1, -1.0, pe)
    v2 = jnp.max(pe2, axis=0, keepdims=True)
    i2 = jnp.min(jnp.where(pe2 == v2, sub8, EPG), axis=0, keepdims=True)
    norm = p_grp * (1.0 / (v1 + v2))
    gate_ref[...] = jnp.where(sub8 == 0, v1 * norm, jnp.where(sub8 == 1, v2 * norm, 0.0))
    e1 = gidx * EPG + i1
    e2 = gidx * EPG + i2

    tri = jnp.where(lax.broadcasted_iota(I32, (chunk, chunk), 0) < lax.broadcasted_iota(I32, (chunk, chunk), 1),
                    1.0, 0.0).astype(BF16)
    sub32 = lax.broadcasted_iota(I32, (N_EXPERTS, chunk), 0)
    count = jnp.zeros((N_EXPERTS, 1), F32)
    for c in range(t_all // chunk):
        sl = slice(c * chunk, (c + 1) * chunk)
        member = jnp.where((sub32 == e1[:, sl]) | (sub32 == e2[:, sl]), 1.0, 0.0)
        pref_scr[:, sl] = jnp.dot(member.astype(BF16), tri, preferred_element_type=F32) + count
        count = count + jnp.sum(member, axis=1, keepdims=True)

    nblk_e = jnp.floor((count + (MOE_BLK - 1)) * (1.0 / MOE_BLK))
    nb_l = jnp.broadcast_to(nblk_e, (N_EXPERTS, lanes))
    hi16 = jnp.floor(nb_l * (1.0 / 16.0))
    lo16 = nb_l - 16.0 * hi16
    ltri = jnp.where(lax.broadcasted_iota(I32, (N_EXPERTS, N_EXPERTS), 1) < lax.broadcasted_iota(I32, (N_EXPERTS, N_EXPERTS), 0),
                     1.0, 0.0).astype(BF16)
    first_blk = (16.0 * jnp.dot(ltri, hi16.astype(BF16), preferred_element_type=F32)
                 + jnp.dot(ltri, lo16.astype(BF16), preferred_element_type=F32))
    end_blk = first_blk + nb_l
    pstart = first_blk[:, 0:1] * float(MOE_BLK)

    sub32t = lax.broadcasted_iota(I32, (N_EXPERTS, t_all), 0)
    slot = pref_scr[...] + pstart
    d1 = jnp.sum(jnp.where(sub32t == e1, slot, 0.0), axis=0, keepdims=True)
    d2 = jnp.sum(jnp.where(sub32t == e2, slot, 0.0), axis=0, keepdims=True)
    dest_ref[...] = jnp.where(sub8 == 0, d1, jnp.where(sub8 == 1, d2, 0.0)).astype(I32)

    blk_i = lax.broadcasted_iota(I32, (N_EXPERTS, n_blk), 1).astype(F32)
    owner = jnp.sum(jnp.where(end_blk[:, 0:1] <= blk_i, 1.0, 0.0), axis=0, keepdims=True)
    owner = jnp.minimum(owner, float(N_EXPERTS - 1))
    n_used = jnp.sum(nblk_e, axis=0, keepdims=True)
    inside = (first_blk[:, 0:1] <= blk_i) & (blk_i < end_blk[:, 0:1])
    left = count - (blk_i - first_blk[:, 0:1]) * float(MOE_BLK)
    rows = jnp.sum(jnp.where(inside, jnp.minimum(left, float(MOE_BLK)), 0.0), axis=0, keepdims=True)
    row8 = lax.broadcasted_iota(I32, (8, n_blk), 0)
    blk_ref[...] = jnp.where(row8 == 0, owner, jnp.where(row8 == 1, n_used, jnp.where(row8 == 2, rows, 0.0))).astype(I32)


def _route(logits_t, n_slots):
    t_all = logits_t.shape[1]
    n_blk = pl.cdiv(n_slots // MOE_BLK, 128) * 128
    vmem = pl.BlockSpec(memory_space=pltpu.VMEM)
    return pl.pallas_call(
        functools.partial(_route_kernel, t_all=t_all, n_blk=n_blk),
        in_specs=[vmem],
        out_specs=[vmem, vmem, vmem],
        out_shape=[
            jax.ShapeDtypeStruct((8, t_all), I32),
            jax.ShapeDtypeStruct((8, t_all), F32),
            jax.ShapeDtypeStruct((8, n_blk), I32),
        ],
        scratch_shapes=[pltpu.VMEM((N_EXPERTS, t_all), F32)],
        compiler_params=pltpu.CompilerParams(vmem_limit_bytes=VMEM_LIMIT),
        name="route",
    )(logits_t)


def _slot_map_kernel(d1_ref, d2_ref, init_hbm, slot_ref, sem, *, t_all, t_pad):
    fill = pltpu.make_async_copy(init_hbm, slot_ref, sem)
    fill.start()
    fill.wait()
    unroll = 8

    def body(j, c):
        for u in range(unroll):
            t = j * unroll + u
            slot_ref[d1_ref[t]] = t
            slot_ref[d2_ref[t]] = t_pad + t
        return c

    lax.fori_loop(0, t_all // unroll, body, 0)


def _slot_map(d1, d2, n_slots, t_pad):
    t_all = d1.shape[0]
    init = t_all + jnp.bitwise_and(jnp.arange(n_slots, dtype=I32), 2 * MOE_BLK - 1)
    smem = pl.BlockSpec(memory_space=pltpu.SMEM)
    return pl.pallas_call(
        functools.partial(_slot_map_kernel, t_all=t_all, t_pad=t_pad),
        in_specs=[smem, smem, pl.BlockSpec(memory_space=pl.ANY)],
        out_specs=smem,
        out_shape=jax.ShapeDtypeStruct((n_slots,), I32),
        scratch_shapes=[pltpu.SemaphoreType.DMA(())],
        name="slot_map",
    )(d1, d2, init)


def _expert_changed(b, blk_e_ref):
    prev = blk_e_ref[jnp.maximum(b - 1, 0)]
    return (b == 0) | (blk_e_ref[b] != prev)


def _gather_start(slot_ref, h_hbm, xbuf, sem, blk, buf, r, t_all, t_pad):
    row = slot_ref[blk * MOE_BLK + r]
    tok = jnp.minimum(jnp.where(row >= t_pad, row - t_pad, row), t_all - 1)
    pltpu.make_async_copy(h_hbm.at[pl.ds(tok, 1), :], xbuf.at[buf, pl.ds(r, 1), :], sem.at[buf]).start()


def _gather_wait(h_hbm, xbuf, sem, buf):
    for r in range(MOE_BLK):
        pltpu.make_async_copy(h_hbm.at[pl.ds(0, 1), :], xbuf.at[buf, pl.ds(r, 1), :], sem.at[buf]).wait()


def _row_steps(rows_ref, b):
    return jnp.right_shift(rows_ref[b] + (MOE_ROW_STEP - 1), MOE_ROW_STEP.bit_length() - 1)


def _moe_up_kernel(blk_e_ref, nused_ref, rows_ref, slot_ref, h_hbm, gffn_ref, wg_ref, wu_ref, hid_ref, xbuf, sem,
                   wg_bf, wu_bf, *, t_all, t_pad):
    b = pl.program_id(0)
    nused = nused_ref[0]
    cur = lax.rem(b, 2)
    n_chunks = 4
    rows_per_chunk = MOE_BLK // n_chunks
    nch = D_EXPERT // n_chunks
    steps = _row_steps(rows_ref, b)

    @pl.when(b == 0)
    def _():
        for r in range(MOE_BLK):
            _gather_start(slot_ref, h_hbm, xbuf, sem, 0, 0, r, t_all, t_pad)

    @pl.when(b < nused)
    def _():
        _gather_wait(h_hbm, xbuf, sem, cur)

        @pl.when(_expert_changed(b, blk_e_ref))
        def _():
            wg_bf[...] = wg_ref[...].astype(BF16)
            wu_bf[...] = wu_ref[...].astype(BF16)

    def compute(m):
        nxt = jnp.minimum(b + 1, nused - 1)
        x = _ffn_norm(xbuf[cur, 0:m, :], gffn_ref[...]).astype(BF16)
        for j in range(n_chunks):
            for r in range(j * rows_per_chunk, (j + 1) * rows_per_chunk):
                _gather_start(slot_ref, h_hbm, xbuf, sem, nxt, 1 - cur, r, t_all, t_pad)
            sl = slice(j * nch, (j + 1) * nch)
            gt = jnp.dot(x, wg_bf[:, sl], preferred_element_type=F32)
            up = jnp.dot(x, wu_bf[:, sl], preferred_element_type=F32)
            hid_ref[0:m, sl] = (gt * (1.0 / (1.0 + jnp.exp(-gt))) * up).astype(BF16)
        if m < MOE_BLK:
            hid_ref[m:, :] = jnp.zeros((MOE_BLK - m, D_EXPERT), BF16)

    for k in range(1, MOE_BLK // MOE_ROW_STEP + 1):
        @pl.when((b < nused) & (steps == k))
        def _(k=k):
            compute(k * MOE_ROW_STEP)

    @pl.when(b == nused - 1)
    def _():
        _gather_wait(h_hbm, xbuf, sem, 1 - cur)

    @pl.when(b >= nused)
    def _():
        hid_ref[...] = jnp.zeros_like(hid_ref)


def _scatter_start(slot_ref, ybuf, y_hbm, sem, blk, buf, r):
    row = slot_ref[blk * MOE_BLK + r]
    pltpu.make_async_copy(ybuf.at[buf, pl.ds(r, 1), :], y_hbm.at[pl.ds(row, 1), :], sem.at[buf]).start()


def _scatter_wait(ybuf, y_hbm, sem, buf):
    for r in range(MOE_BLK):
        pltpu.make_async_copy(ybuf.at[buf, pl.ds(r, 1), :], y_hbm.at[pl.ds(0, 1), :], sem.at[buf]).wait()


def _moe_down_kernel(blk_e_ref, nused_ref, rows_ref, slot_ref, hid_ref, wd_ref, y_hbm, ybuf, sem, wd_bf, *, t_all, t_pad):
    b = pl.program_id(0)
    nused = nused_ref[0]
    cur = lax.rem(b, 2)
    n_chunks = 4
    rows_per_chunk = MOE_BLK // n_chunks
    nch = D_MODEL // n_chunks
    steps = _row_steps(rows_ref, b)

    @pl.when(b == 0)
    def _():
        ybuf[...] = jnp.zeros(ybuf.shape, F32)
        for half in range(2):
            for part in range(2):
                row0 = half * t_pad + t_all + part * MOE_BLK
                cp = pltpu.make_async_copy(ybuf.at[0], y_hbm.at[pl.ds(row0, MOE_BLK), :], sem.at[0])
                cp.start()
                cp.wait()

    @pl.when((b >= 2) & (b <= nused))
    def _():
        _scatter_wait(ybuf, y_hbm, sem, cur)

    def compute(m, scatter_previous):
        h = hid_ref[0:m, :]
        for j in range(n_chunks):
            if scatter_previous:
                for r in range(j * rows_per_chunk, (j + 1) * rows_per_chunk):
                    _scatter_start(slot_ref, ybuf, y_hbm, sem, b - 1, 1 - cur, r)
            sl = slice(j * nch, (j + 1) * nch)
            ybuf[cur, 0:m, sl] = jnp.dot(h, wd_bf[:, sl], preferred_element_type=F32)

    @pl.when(b == 0)
    def _():
        wd_bf[...] = wd_ref[...].astype(BF16)
        compute(MOE_BLK, False)

    @pl.when((b >= 1) & (b < nused) & _expert_changed(b, blk_e_ref))
    def _():
        wd_bf[...] = wd_ref[...].astype(BF16)

    for k in range(1, MOE_BLK // MOE_ROW_STEP + 1):
        @pl.when((b >= 1) & (b < nused) & (steps == k))
        def _(k=k):
            compute(k * MOE_ROW_STEP, True)

    @pl.when(b == nused)
    def _():
        for r in range(MOE_BLK):
            _scatter_start(slot_ref, ybuf, y_hbm, sem, b - 1, 1 - cur, r)
        _scatter_wait(ybuf, y_hbm, sem, 1 - cur)


def _blk(b, nused_ref):
    return jnp.minimum(b, nused_ref[0] - 1)


def _moe(x1, g_ffn, slot, blk_e, n_used, blk_rows, w_gate, w_up, w_down, t_pad):
    t_all = x1.shape[0]
    n_slots = slot.shape[0]
    n_blk = n_slots // MOE_BLK
    hid = pl.pallas_call(
        functools.partial(_moe_up_kernel, t_all=t_all, t_pad=t_pad),
        grid_spec=pltpu.PrefetchScalarGridSpec(
            num_scalar_prefetch=4,
            grid=(n_blk,),
            in_specs=[
                pl.BlockSpec(memory_space=pl.ANY),
                pl.BlockSpec((1, D_MODEL), lambda b, be, nu, rw, sl: (0, 0)),
                pl.BlockSpec((None, D_MODEL, D_EXPERT), lambda b, be, nu, rw, sl: (be[_blk(b, nu)], 0, 0)),
                pl.BlockSpec((None, D_MODEL, D_EXPERT), lambda b, be, nu, rw, sl: (be[_blk(b, nu)], 0, 0)),
            ],
            out_specs=pl.BlockSpec((MOE_BLK, D_EXPERT), lambda b, be, nu, rw, sl: (b, 0)),
            scratch_shapes=[pltpu.VMEM((2, MOE_BLK, D_MODEL), F32), pltpu.SemaphoreType.DMA((2,)),
                            pltpu.VMEM((D_MODEL, D_EXPERT), BF16), pltpu.VMEM((D_MODEL, D_EXPERT), BF16)],
        ),
        out_shape=jax.ShapeDtypeStruct((n_slots, D_EXPERT), BF16),
        compiler_params=_cparams(("arbitrary",), MOE_VMEM_LIMIT),
        name="moe_gate_up",
    )(blk_e, n_used, blk_rows, slot, x1, g_ffn.reshape(1, D_MODEL), w_gate, w_up)
    return pl.pallas_call(
        functools.partial(_moe_down_kernel, t_all=t_all, t_pad=t_pad),
        grid_spec=pltpu.PrefetchScalarGridSpec(
            num_scalar_prefetch=4,
            grid=(n_blk,),
            in_specs=[
                pl.BlockSpec((MOE_BLK, D_EXPERT), lambda b, be, nu, rw, sl: (_blk(b, nu), 0)),
                pl.BlockSpec((None, D_EXPERT, D_MODEL), lambda b, be, nu, rw, sl: (be[_blk(b, nu)], 0, 0)),
            ],
            out_specs=pl.BlockSpec(memory_space=pl.ANY),
            scratch_shapes=[pltpu.VMEM((2, MOE_BLK, D_MODEL), F32), pltpu.SemaphoreType.DMA((2,)),
                            pltpu.VMEM((D_EXPERT, D_MODEL), BF16)],
        ),
        out_shape=jax.ShapeDtypeStruct((2 * t_pad, D_MODEL), F32),
        compiler_params=_cparams(("arbitrary",), VMEM_LIMIT),
        name="moe_down",
    )(blk_e, n_used, blk_rows, slot, hid, w_down)


def _combine_body(x1_ref, ya_ref, yb_ref, gate_ref, g_ref, out_ref):
    gt = gate_ref[...]
    x2 = x1_ref[...] + (ya_ref[...] * gt[:, 0:1] + yb_ref[...] * gt[:, 1:2])
    out_ref[...] = _rms_scale(x2) * g_ref[...]


def _combine_kernel(x1_ref, ya_ref, yb_ref, gate_ref, g_ref, outp_ref, outs_ref, *, n_prompt_blocks):
    i = pl.program_id(0)

    @pl.when(i < n_prompt_blocks)
    def _():
        _combine_body(x1_ref, ya_ref, yb_ref, gate_ref, g_ref, outp_ref)

    @pl.when(i >= n_prompt_blocks)
    def _():
        _combine_body(x1_ref, ya_ref, yb_ref, gate_ref, g_ref, outs_ref)


def _combine(x1, gates_t, g_final, y2, tp, ts, t_pad):
    tm = ROW_TILE
    npb, nsb = tp // tm, ts // tm
    second = t_pad // tm
    return pl.pallas_call(
        functools.partial(_combine_kernel, n_prompt_blocks=npb),
        grid=(npb + nsb,),
        in_specs=[
            pl.BlockSpec((tm, D_MODEL), lambda i: (i, 0)),
            pl.BlockSpec((tm, D_MODEL), lambda i: (i, 0)),
            pl.BlockSpec((tm, D_MODEL), lambda i: (second + i, 0)),
            pl.BlockSpec((tm, 2), lambda i: (i, 0)),
            pl.BlockSpec((1, D_MODEL), lambda i: (0, 0)),
        ],
        out_specs=[
            pl.BlockSpec((tm, D_MODEL), lambda i: (jnp.minimum(i, npb - 1), 0)),
            pl.BlockSpec((tm, D_MODEL), lambda i: (jnp.maximum(i - npb, 0), 0)),
        ],
        out_shape=[jax.ShapeDtypeStruct((tp, D_MODEL), F32), jax.ShapeDtypeStruct((ts, D_MODEL), F32)],
        compiler_params=_cparams(("arbitrary",), VMEM_LIMIT),
        name="combine_final_norm",
    )(x1, y2, y2, gates_t, g_final.reshape(1, D_MODEL))


def _layer(xp, xs, g_attn, w_in, g_mix, sink_a, rpb_b, w_out, g_ffn, w_rg, w_re, w_gate, w_up, w_down, g_final):
    tp, ts = xp.shape[0], xs.shape[0]
    t_all = tp + ts
    bounds = (0, tp, t_all)
    s0, s1, s2, s3, s4 = QA_W, QA_W + KVA_W, QA_W + 2 * KVA_W, QA_W + 2 * KVA_W + B_W, QA_W + 2 * KVA_W + 2 * B_W
    w_in_p = jnp.concatenate([w_in[:, :s0], w_in[:, s2:s3], w_in[:, s3:s4], w_in[:, s4:], w_in[:, s0:s1], w_in[:, s1:s2]],
                             axis=1).astype(BF16)
    w_router = jnp.concatenate([w_re.T, w_rg.T, jnp.zeros((ROUTER_ROWS - N_EXPERTS - N_GROUPS, D_MODEL), F32)], axis=0)

    proj = _in_proj(xp, xs, g_attn, w_in_p)
    o_a = _window_attention(proj, sink_a, bounds)
    o_b = _neighborhood_attention(proj, _bias_table(rpb_b), bounds)
    x1, logits_t = _out_proj(xp, xs, o_a, o_b, g_mix, w_out.astype(BF16), g_ffn, w_router)

    n_slots = 2 * t_all + N_EXPERTS * MOE_BLK
    n_blk = n_slots // MOE_BLK
    t_pad = t_all + 2 * MOE_BLK
    dest, gate, blk = _route(logits_t, n_slots)
    slot = _slot_map(dest[0], dest[1], n_slots, t_pad)
    y2 = _moe(x1, g_ffn, slot, blk[0, :n_blk], blk[1, :1], blk[2, :n_blk], w_gate, w_up, w_down, t_pad)
    return _combine(x1, gate[:2].T, g_final, y2, tp, ts, t_pad)


def kernel(x_prompt, x_sample, g_attn, w_in, g_mix, sink_a, rpb_b, w_out, g_ffn, w_router_group, w_router_expert,
           w_gate, w_up, w_down, g_final):
    assert x_prompt.shape[0] == 1 and x_sample.shape[0] == 1 and g_attn.shape[0] == 1
    yp, ys = _layer(x_prompt[0], x_sample[0], g_attn[0], w_in[0], g_mix[0], sink_a[0], rpb_b[0], w_out[0], g_ffn[0],
                    w_router_group[0], w_router_expert[0], w_gate[0], w_up[0], w_down[0], g_final)
    return yp[None], ys[None]
```
